```python
import jax
import jax.numpy as jnp
from jax import lax
import numpy as np

D_MODEL = 1024
BATCH = 1
SEQ = 16384
DEPTH = 4

N_EVEN = (DEPTH + 1) // 2
N_ODD = DEPTH // 2
NORM_EPS = 1e-6
D_FF = 4 * D_MODEL

LRU_WIDTH = D_MODEL // 2
LRU_BLOCKS = 8
LRU_BLOCK = LRU_WIDTH // LRU_BLOCKS
CONV_WIDTH = 4
LRU_C = 8.0

RWKV_WIDTH = D_MODEL // 2
RWKV_HEAD = 64
RWKV_HEADS = RWKV_WIDTH // RWKV_HEAD
DECAY_LORA = 64
AAA_LORA = 64
GATE_LORA = 128
RWKV_LN_EPS = 64e-5
RWKV_COLS = 3 * RWKV_WIDTH + DECAY_LORA + AAA_LORA + GATE_LORA
AB_IN = 2 * LRU_WIDTH + RWKV_COLS

RET_HEADS = 4
RET_QK = D_MODEL
RET_V = 2 * D_MODEL
RET_DK = RET_QK // RET_HEADS
RET_DV = RET_V // RET_HEADS
RET_CHUNK = 128
RET_EPS = 1e-6
ROPE_BASE = 10000.0
RET_IN = 2 * RET_QK + 2 * RET_V

kernel_name = "hybrid_rglru_rwkv7_retention_trunk"


def rms_norm(x, g, eps=NORM_EPS):
    xf = x.astype(jnp.float32)
    y = xf * lax.rsqrt(jnp.mean(xf * xf, axis=-1, keepdims=True) + eps)
    return (y * g.astype(jnp.float32)).astype(x.dtype)


def sq_relu_mlp(x, w_in, w_out):
    return jnp.square(jax.nn.relu(x @ w_in)) @ w_out


def causal_depthwise_conv(u, w, b):
    k = w.shape[0]
    out = lax.conv_general_dilated(
        u, w[:, None, :], window_strides=(1,), padding=[(k - 1, 0)],
        dimension_numbers=("NWC", "WIO", "NWC"), feature_group_count=u.shape[-1])
    return out + b


def rg_lru(u, wa, ba, wx, bx, lam):
    b_, t, w = u.shape
    ub = u.reshape(b_, t, LRU_BLOCKS, LRU_BLOCK)
    r = jax.nn.sigmoid((jnp.einsum("btnc,ncd->btnd", ub, wa).reshape(b_, t, w) + ba).astype(jnp.float32))
    i = jax.nn.sigmoid((jnp.einsum("btnc,ncd->btnd", ub, wx).reshape(b_, t, w) + bx).astype(jnp.float32))
    log_a = -LRU_C * r * jax.nn.softplus(-lam.astype(jnp.float32))
    a = jnp.exp(log_a)
    drive = jnp.sqrt(-jnp.expm1(2.0 * log_a)) * (i * u.astype(jnp.float32))

    def combine(lhs, rhs):
        a1, b1 = lhs
        a2, b2 = rhs
        return a1 * a2, a2 * b1 + b2

    _, h = lax.associative_scan(combine, (a, drive), axis=1)
    return h


def rwkv7_time_mix(cols, w0, w2, a0, a2, g2, k_k, k_a, r_k, ln_w, ln_b):
    f32 = jnp.float32
    cols = cols.astype(f32)
    b_, t, _ = cols.shape
    hs = (b_, t, RWKV_HEADS, RWKV_HEAD)
    o = RWKV_WIDTH
    r = cols[..., 0:o]
    k = cols[..., o:2 * o]
    v = cols[..., 2 * o:3 * o]
    dw = cols[..., 3 * o:3 * o + DECAY_LORA]
    da = cols[..., 3 * o + DECAY_LORA:3 * o + DECAY_LORA + AAA_LORA]
    dg = cols[..., 3 * o + DECAY_LORA + AAA_LORA:]

    w_log = -jax.nn.softplus(-(w0 + jnp.tanh(dw) @ w2)) - 0.5
    decay = jnp.exp(-jnp.exp(w_log))
    a = jax.nn.sigmoid(a0 + da @ a2)
    g = jax.nn.sigmoid(dg) @ g2
    kk = (k * k_k).reshape(hs)
    kk = kk / jnp.maximum(jnp.sqrt(jnp.sum(kk * kk, axis=-1, keepdims=True)), 1e-12)
    k = k * (1.0 + (a - 1.0) * k_a)

    r_h, k_h, v_h = r.reshape(hs), k.reshape(hs), v.reshape(hs)
    w_h, a_h = decay.reshape(hs), a.reshape(hs)
    z_h = -kk
    b_h = kk * a_h

    def step(s, inp):
        r_t, w_t, k_t, v_t, z_t, bb_t = inp
        sz = jnp.einsum("bhij,bhj->bhi", s, z_t)
        s = s * w_t[:, :, None, :] + sz[..., None] * bb_t[:, :, None, :] + v_t[..., None] * k_t[:, :, None, :]
        return s, jnp.einsum("bhij,bhj->bhi", s, r_t)

    tm = lambda arr: jnp.moveaxis(arr, 1, 0)
    s0 = jnp.zeros((b_, RWKV_HEADS, RWKV_HEAD, RWKV_HEAD), f32)
    _, y = lax.scan(step, s0, (tm(r_h), tm(w_h), tm(k_h), tm(v_h), tm(z_h), tm(b_h)))
    y = jnp.moveaxis(y, 0, 1)

    mu = jnp.mean(y, axis=-1, keepdims=True)
    var = jnp.mean(jnp.square(y - mu), axis=-1, keepdims=True)
    yn = ((y - mu) * lax.rsqrt(var + RWKV_LN_EPS)).reshape(b_, t, RWKV_WIDTH) * ln_w + ln_b
    bonus = jnp.sum(r_h * k_h * r_k.reshape(RWKV_HEADS, RWKV_HEAD), axis=-1, keepdims=True) * v_h
    return (yn + bonus.reshape(b_, t, RWKV_WIDTH)) * g


def mixer_ab(h, w_in, w_out, conv_w, conv_b, wa, ba, wx, bx, lam,
             mu, w0, w2, a0, a2, g2, k_k, k_a, r_k, ln_w, ln_b):
    u = h @ w_in
    ax = u[..., :LRU_WIDTH]
    ag = u[..., LRU_WIDTH:2 * LRU_WIDTH]
    bc = u[..., 2 * LRU_WIDTH:]
    ax = causal_depthwise_conv(ax, conv_w, conv_b)
    ya = rg_lru(ax, wa, ba, wx, bx, lam) * jax.nn.gelu(ag.astype(jnp.float32))
    prev = jnp.pad(bc, ((0, 0), (1, 0), (0, 0)))[:, :-1]
    bc = bc + (prev - bc) * mu
    yb = rwkv7_time_mix(bc, w0, w2, a0, a2, g2, k_k, k_a, r_k, ln_w, ln_b)
    y = jnp.concatenate([ya, yb], axis=-1).astype(h.dtype)
    return y @ w_out


def rotary(x, pos):
    half = x.shape[-1] // 2
    inv = 1.0 / (ROPE_BASE ** (jnp.arange(half, dtype=jnp.float32) / half))
    ang = pos.astype(jnp.float32)[:, None] * inv[None, :]
    cos = jnp.cos(ang)[:, None, :]
    sin = jnp.sin(ang)[:, None, :]
    x1, x2 = x[..., :half], x[..., half:]
    return jnp.concatenate([x1 * cos - x2 * sin, x2 * cos + x1 * sin], axis=-1)


def retention_chunkwise(q, k, v):
    f32 = jnp.float32
    b_, t, h, _ = q.shape
    c = RET_CHUNK
    nc = t // c
    log_g = jnp.log1p(-jnp.power(2.0, -5.0 - jnp.arange(h, dtype=f32)))
    idx = jnp.arange(c, dtype=f32)
    rel = idx[:, None] - idx[None, :]
    intra = jnp.where(rel >= 0, jnp.exp(log_g[:, None, None] * jnp.maximum(rel, 0.0)), 0.0)
    q_dec = jnp.exp(log_g[:, None] * (idx + 1.0)).T[None, :, :, None]
    k_dec = jnp.exp(log_g[:, None] * (c - 1.0 - idx)).T[None, :, :, None]
    chunk_dec = jnp.exp(log_g * c)[None, :, None, None]

    def to_chunks(arr):
        return jnp.moveaxis(arr.reshape(b_, nc, c, h, arr.shape[-1]), 1, 0)

    def step(state, inp):
        qc, kc, vc = inp
        s = jnp.einsum("bihd,bjhd->bhij", qc, kc) * intra[None]
        inner = jnp.einsum("bhij,bjhe->bihe", s, vc)
        cross = jnp.einsum("bihd,bhde->bihe", qc, state) * q_dec
        state = state * chunk_dec + jnp.einsum("bjhd,bjhe->bhde", kc * k_dec, vc)
        return state, inner + cross

    s0 = jnp.zeros((b_, h, q.shape[-1], v.shape[-1]), f32)
    _, out = lax.scan(step, s0, (to_chunks(q), to_chunks(k), to_chunks(v)))
    return jnp.moveaxis(out, 0, 1).reshape(b_, t, h, v.shape[-1])


def mixer_ret(h, w_in, w_out, pos):
    f32 = jnp.float32
    b_, t, _ = h.shape
    u = h @ w_in
    q = u[..., :RET_QK].reshape(b_, t, RET_HEADS, RET_DK).astype(f32)
    k = u[..., RET_QK:2 * RET_QK].reshape(b_, t, RET_HEADS, RET_DK).astype(f32)
    v = u[..., 2 * RET_QK:2 * RET_QK + RET_V].reshape(b_, t, RET_HEADS, RET_DV).astype(f32)
    g = u[..., 2 * RET_QK + RET_V:].astype(f32)
    q = rotary(q, pos)
    k = rotary(k, pos) * (RET_DK ** -0.5)
    y = retention_chunkwise(q, k, v)
    y = y * lax.rsqrt(jnp.mean(y * y, axis=-1, keepdims=True) + RET_EPS)
    y = jax.nn.silu(g) * y.reshape(b_, t, RET_V)
    return y.astype(h.dtype) @ w_out


def setup_inputs(seed: int = 0) -> dict:
    key = jax.random.key(seed)
    ks = jax.random.split(key, 32)
    f32 = jnp.float32

    def nrm(k, shape, scale):
        return jax.random.normal(k, shape, f32) * scale

    def gain(k, shape):
        return 1.0 + 0.05 * jax.random.normal(k, shape, f32)

    lam_u = jax.random.uniform(ks[14], (N_EVEN, LRU_WIDTH), f32, 0.9, 0.999)
    a_base = lam_u ** (1.0 / LRU_C)
    lru_lambda = jnp.log(a_base) - jnp.log1p(-a_base)

    return {
        "x": nrm(ks[0], (BATCH, SEQ, D_MODEL), 1.0),
        "norm_mix_pre": gain(ks[1], (DEPTH, D_MODEL)),
        "norm_mix_post": gain(ks[2], (DEPTH, D_MODEL)),
        "norm_ffn_pre": gain(ks[3], (DEPTH, D_MODEL)),
        "norm_ffn_post": gain(ks[4], (DEPTH, D_MODEL)),
        "ffn_w_in": nrm(ks[5], (DEPTH, D_MODEL, D_FF), D_MODEL ** -0.5),
        "ffn_w_out": nrm(ks[6], (DEPTH, D_FF, D_MODEL), D_FF ** -0.5),
        "ab_w_in": nrm(ks[7], (N_EVEN, D_MODEL, AB_IN), D_MODEL ** -0.5),
        "ab_w_out": nrm(ks[8], (N_EVEN, LRU_WIDTH + RWKV_WIDTH, D_MODEL), (LRU_WIDTH + RWKV_WIDTH) ** -0.5),
        "lru_conv_w": nrm(ks[9], (N_EVEN, CONV_WIDTH, LRU_WIDTH), CONV_WIDTH ** -0.5),
        "lru_conv_b": nrm(ks[10], (N_EVEN, LRU_WIDTH), 0.01),
        "lru_wa": nrm(ks[11], (N_EVEN, LRU_BLOCKS, LRU_BLOCK, LRU_BLOCK), LRU_BLOCK ** -0.5),
        "lru_ba": nrm(ks[12], (N_EVEN, LRU_WIDTH), 0.01),
        "lru_wx": nrm(ks[13], (N_EVEN, LRU_BLOCKS, LRU_BLOCK, LRU_BLOCK), LRU_BLOCK ** -0.5),
        "lru_bx": nrm(ks[15], (N_EVEN, LRU_WIDTH), 0.01),
        "lru_lambda": lru_lambda,
        "rwkv_mu": jax.random.uniform(ks[16], (N_EVEN, RWKV_COLS), f32, 0.0, 1.0),
        "rwkv_w0": jax.random.uniform(ks[17], (N_EVEN, RWKV_WIDTH), f32, -6.0, -1.0),
        "rwkv_w2": nrm(ks[18], (N_EVEN, DECAY_LORA, RWKV_WIDTH), 0.5 * DECAY_LORA ** -0.5),
        "rwkv_a0": nrm(ks[19], (N_EVEN, RWKV_WIDTH), 0.1),
        "rwkv_a2": nrm(ks[20], (N_EVEN, AAA_LORA, RWKV_WIDTH), 0.5 * AAA_LORA ** -0.5),
        "rwkv_g2": nrm(ks[21], (N_EVEN, GATE_LORA, RWKV_WIDTH), GATE_LORA ** -0.5),
        "rwkv_k_k": 0.85 + nrm(ks[22], (N_EVEN, RWKV_WIDTH), 0.05),
        "rwkv_k_a": 1.0 + nrm(ks[23], (N_EVEN, RWKV_WIDTH), 0.05),
        "rwkv_r_k": nrm(ks[24], (N_EVEN, RWKV_WIDTH), 0.1),
        "rwkv_ln_w": gain(ks[25], (N_EVEN, RWKV_WIDTH)),
        "rwkv_ln_b": nrm(ks[26], (N_EVEN, RWKV_WIDTH), 0.01),
        "ret_w_in": nrm(ks[27], (N_ODD, D_MODEL, RET_IN), D_MODEL ** -0.5),
        "ret_w_out": nrm(ks[28], (N_ODD, RET_V, D_MODEL), RET_V ** -0.5),
    }


def reference(x, norm_mix_pre, norm_mix_post, norm_ffn_pre, norm_ffn_post, ffn_w_in, ffn_w_out,
              ab_w_in, ab_w_out, lru_conv_w, lru_conv_b, lru_wa, lru_ba, lru_wx, lru_bx, lru_lambda,
              rwkv_mu, rwkv_w0, rwkv_w2, rwkv_a0, rwkv_a2, rwkv_g2, rwkv_k_k, rwkv_k_a, rwkv_r_k,
              rwkv_ln_w, rwkv_ln_b, ret_w_in, ret_w_out):
    pos = jnp.arange(x.shape[1], dtype=jnp.int32)
    h = x
    for layer in range(DEPTH):
        hn = rms_norm(h, norm_mix_pre[layer])
        if layer % 2 == 0:
            e = layer // 2
            m = mixer_ab(hn, ab_w_in[e], ab_w_out[e], lru_conv_w[e], lru_conv_b[e],
                         lru_wa[e], lru_ba[e], lru_wx[e], lru_bx[e], lru_lambda[e],
                         rwkv_mu[e], rwkv_w0[e], rwkv_w2[e], rwkv_a0[e], rwkv_a2[e], rwkv_g2[e],
                         rwkv_k_k[e], rwkv_k_a[e], rwkv_r_k[e], rwkv_ln_w[e], rwkv_ln_b[e])
        else:
            o = layer // 2
            m = mixer_ret(hn, ret_w_in[o], ret_w_out[o], pos)
        h = h + rms_norm(m, norm_mix_post[layer])
        f = sq_relu_mlp(rms_norm(h, norm_ffn_pre[layer]), ffn_w_in[layer], ffn_w_out[layer])
        h = h + rms_norm(f, norm_ffn_post[layer])
    return h
```

```python
import functools
import math

import jax
import jax.numpy as jnp
from jax import lax
from jax.experimental import pallas as pl
from jax.experimental.pallas import tpu as pltpu

F32 = jnp.float32
BF16 = jnp.bfloat16

D_MODEL = 1024
D_FF = 4 * D_MODEL
NORM_EPS = 1e-6

LRU_WIDTH = 512
LRU_BLOCKS = 8
LRU_C = 8.0

RWKV_WIDTH = 512
RWKV_HEAD = 64
RWKV_LN_EPS = 64e-5
RWKV_COLS = 3 * RWKV_WIDTH + 64 + 64 + 128
RWKV_CHUNK = 64
RWKV_PAIR = 2 * RWKV_HEAD

RET_HEADS = 4
RET_DK = 256
RET_DV = 512
RET_EPS = 1e-6
ROPE_BASE = 10000.0

MIX_ROWS = 256
MM_ROWS = 512
FF_CHUNK = 1024
VMEM_LIMIT = 56 * 1024 * 1024


def _bdot(a, b):
    return jnp.dot(a.astype(BF16), b.astype(BF16), preferred_element_type=F32)


def _bdot_nt(a, b):
    return lax.dot_general(a.astype(BF16), b.astype(BF16), (((1,), (1,)), ((), ())), preferred_element_type=F32)


def _bdot_tn(a, b):
    return lax.dot_general(a.astype(BF16), b.astype(BF16), (((0,), (0,)), ((), ())), preferred_element_type=F32)


def _split(x):
    hi = x.astype(BF16)
    lo = (x - hi.astype(F32)).astype(BF16)
    return hi, lo


def _dot_split(a, b):
    ah, al = _split(a)
    bh, bl = _split(b)
    d = lambda x, y: jnp.dot(x, y, preferred_element_type=F32)
    return d(ah, bh) + (d(ah, bl) + d(al, bh))


def _dot_exact_lhs(a_bf16, b):
    b0 = b.astype(BF16)
    r1 = b - b0.astype(F32)
    b1 = r1.astype(BF16)
    b2 = (r1 - b1.astype(F32)).astype(BF16)
    d = lambda y: jnp.dot(a_bf16, y, preferred_element_type=F32)
    return d(b0) + (d(b1) + d(b2))


def _rms(x, g):
    return x * lax.rsqrt(jnp.mean(x * x, axis=-1, keepdims=True) + NORM_EPS) * g


def _softplus(x):
    return jnp.maximum(x, 0.0) + jnp.log1p(jnp.exp(-jnp.abs(x)))


def _params(n_axes):
    return pltpu.CompilerParams(dimension_semantics=("arbitrary",) * n_axes, vmem_limit_bytes=VMEM_LIMIT)


def _norm_matmul_kernel(x_ref, g_ref, w_ref, o_ref):
    y = _rms(x_ref[...], g_ref[...])
    o_ref[...] = jnp.dot(y.astype(BF16), w_ref[...], preferred_element_type=F32)


def _norm_matmul(h, g, w, tn):
    t, d = h.shape
    n = w.shape[1]
    return pl.pallas_call(
        _norm_matmul_kernel,
        grid=(n // tn, t // MM_ROWS),
        in_specs=[
            pl.BlockSpec((MM_ROWS, d), lambda j, i: (i, 0)),
            pl.BlockSpec((1, d), lambda j, i: (0, 0)),
            pl.BlockSpec((d, tn), lambda j, i: (0, j)),
        ],
        out_specs=pl.BlockSpec((MM_ROWS, tn), lambda j, i: (i, j)),
        out_shape=jax.ShapeDtypeStruct((t, n), F32),
        compiler_params=_params(2),
        name="norm_matmul",
    )(h, g, w)


def _proj_norm_res_kernel(n_in, *refs):
    y_refs = refs[:n_in]
    w_refs = refs[n_in:2 * n_in]
    g_ref, h_ref, o_ref = refs[2 * n_in:]
    m = jnp.dot(y_refs[0][...].astype(BF16), w_refs[0][...], preferred_element_type=F32)
    for y_ref, w_ref in zip(y_refs[1:], w_refs[1:]):
        m = m + jnp.dot(y_ref[...].astype(BF16), w_ref[...], preferred_element_type=F32)
    o_ref[...] = h_ref[...] + _rms(m, g_ref[...])


def _proj_norm_res(ys, ws, g, h):
    t, d = h.shape
    n_in = len(ys)
    in_specs = [pl.BlockSpec((MM_ROWS, y.shape[1]), lambda i: (i, 0)) for y in ys]
    in_specs += [pl.BlockSpec(w.shape, lambda i: (0, 0)) for w in ws]
    in_specs += [pl.BlockSpec((1, d), lambda i: (0, 0)), pl.BlockSpec((MM_ROWS, d), lambda i: (i, 0))]
    return pl.pallas_call(
        functools.partial(_proj_norm_res_kernel, n_in),
        grid=(t // MM_ROWS,),
        in_specs=in_specs,
        out_specs=pl.BlockSpec((MM_ROWS, d), lambda i: (i, 0)),
        out_shape=jax.ShapeDtypeStruct((t, d), F32),
        compiler_params=_params(1),
        name="proj_norm_res",
    )(*ys, *ws, g, h)


def _ffn_kernel(h_ref, gpre_ref, win_ref, wout_ref, gpost_ref, o_ref):
    h = h_ref[...]
    xn = _rms(h, gpre_ref[...]).astype(BF16)
    acc = jnp.zeros(h.shape, F32)
    for c in range(D_FF // FF_CHUNK):
        cols = slice(c * FF_CHUNK, (c + 1) * FF_CHUNK)
        a = jnp.dot(xn, win_ref[:, cols], preferred_element_type=F32)
        a = jnp.square(jnp.maximum(a, 0.0))
        acc = acc + jnp.dot(a.astype(BF16), wout_ref[cols, :], preferred_element_type=F32)
    o_ref[...] = h + _rms(acc, gpost_ref[...])


def _ffn(h, gpre, win, wout, gpost):
    t, d = h.shape
    return pl.pallas_call(
        _ffn_kernel,
        grid=(t // MM_ROWS,),
        in_specs=[
            pl.BlockSpec((MM_ROWS, d), lambda i: (i, 0)),
            pl.BlockSpec((1, d), lambda i: (0, 0)),
            pl.BlockSpec((d, D_FF), lambda i: (0, 0)),
            pl.BlockSpec((D_FF, d), lambda i: (0, 0)),
            pl.BlockSpec((1, d), lambda i: (0, 0)),
        ],
        out_specs=pl.BlockSpec((MM_ROWS, d), lambda i: (i, 0)),
        out_shape=jax.ShapeDtypeStruct((t, d), F32),
        compiler_params=_params(1),
        name="ffn",
    )(h, gpre, win, wout, gpost)


def _lru_kernel(ax_ref, ag_ref, cw_ref, cb_ref, wa_ref, ba_ref, wx_ref, bx_ref, lam_ref, o_ref, xc_ref, hc_ref):
    tb = MIX_ROWS

    @pl.when(pl.program_id(0) == 0)
    def _():
        xc_ref[...] = jnp.zeros_like(xc_ref)
        hc_ref[...] = jnp.zeros_like(hc_ref)

    x = ax_ref[...]
    row = lax.broadcasted_iota(jnp.int32, (tb, 1), 0)
    xc = xc_ref[...]
    cw = cw_ref[...]
    u = x * cw[3:4, :] + cb_ref[...]
    for d in (1, 2, 3):
        head = jnp.concatenate([pltpu.roll(xc, d, axis=0)] * (tb // 8), axis=0)
        shifted = jnp.where(row < d, head, pltpu.roll(x, d, axis=0))
        u = u + shifted * cw[3 - d:4 - d, :]
    xc_ref[...] = x[tb - 8:tb, :]

    r = jax.nn.sigmoid(_bdot(u, wa_ref[...]) + ba_ref[...])
    ig = jax.nn.sigmoid(_bdot(u, wx_ref[...]) + bx_ref[...])
    log_a = (-LRU_C) * r * _softplus(-lam_ref[...])
    a = jnp.exp(log_a)
    b = jnp.sqrt(-jnp.tanh(log_a) * (a * a + 1.0)) * (ig * u)

    d = 1
    while d < tb:
        keep = row >= d
        a_sh = jnp.where(keep, pltpu.roll(a, d, axis=0), 1.0)
        b_sh = jnp.where(keep, pltpu.roll(b, d, axis=0), 0.0)
        b = a * b_sh + b
        a = a * a_sh
        d *= 2
    h = a * hc_ref[0:1, :] + b
    hc_ref[0:1, :] = h[tb - 1:tb, :]
    o_ref[...] = h * jax.nn.gelu(ag_ref[...])


def _lru(u_a, cw, cb, wa_bd, ba, wx_bd, bx, lam):
    t = u_a.shape[0]
    w = LRU_WIDTH
    vec = pl.BlockSpec((1, w), lambda i: (0, 0))
    mat = pl.BlockSpec((w, w), lambda i: (0, 0))
    return pl.pallas_call(
        _lru_kernel,
        grid=(t // MIX_ROWS,),
        in_specs=[
            pl.BlockSpec((MIX_ROWS, w), lambda i: (i, 0)),
            pl.BlockSpec((MIX_ROWS, w), lambda i: (i, 1)),
            pl.BlockSpec((4, w), lambda i: (0, 0)),
            vec, mat, vec, mat, vec, vec,
        ],
        out_specs=pl.BlockSpec((MIX_ROWS, w), lambda i: (i, 0)),
        out_shape=jax.ShapeDtypeStruct((t, w), F32),
        scratch_shapes=[pltpu.VMEM((8, w), F32), pltpu.VMEM((8, w), F32)],
        compiler_params=_params(1),
        name="rg_lru",
    )(u_a, u_a, cw, cb, wa_bd, ba, wx_bd, bx, lam)


def _rwkv_kernel(bc_ref, mu_ref, w0_ref, w2_ref, a0_ref, a2_ref, g2_ref, kk_ref, ka_ref, rk_ref, lnw_ref, lnb_ref,
                 seg_ref, o_ref, carry_ref, st_ref, r_s, k_s, v_s, z_s, b_s, lw_s, y_s):
    tb = MIX_ROWS
    c_len = RWKV_CHUNK
    pw = RWKV_PAIR
    n_pairs = RWKV_WIDTH // pw
    w = RWKV_WIDTH

    @pl.when(pl.program_id(0) == 0)
    def _():
        carry_ref[...] = jnp.zeros_like(carry_ref)
        st_ref[...] = jnp.zeros_like(st_ref)

    x = bc_ref[...]
    row = lax.broadcasted_iota(jnp.int32, (tb, 1), 0)
    prev = jnp.where(row == 0, carry_ref[0:1, :], pltpu.roll(x, 1, axis=0))
    carry_ref[0:1, :] = x[tb - 1:tb, :]
    xs = x + (prev - x) * mu_ref[...]

    seg = seg_ref[...]
    r = xs[:, 0:w]
    k = xs[:, w:2 * w]
    v = xs[:, 2 * w:3 * w]
    dw = xs[:, 3 * w:3 * w + 64]
    da = xs[:, 3 * w + 64:3 * w + 128]
    dg = xs[:, 3 * w + 128:3 * w + 256]
    w_log = -_softplus(-(w0_ref[...] + _bdot(jnp.tanh(dw), w2_ref[...]))) - 0.5
    lw = -jnp.exp(w_log)
    a = jax.nn.sigmoid(a0_ref[...] + _bdot(da, a2_ref[...]))
    g = _bdot(jax.nn.sigmoid(dg), g2_ref[...])
    kk = k * kk_ref[...]
    kk = kk / jnp.maximum(jnp.sqrt(_bdot(kk * kk, seg)), 1e-12)
    k = k * (1.0 + (a - 1.0) * ka_ref[...])
    r_s[...] = r
    k_s[...] = k
    v_s[...] = v
    z_s[...] = -kk
    b_s[...] = kk * a
    lw_s[...] = lw

    ri = lax.broadcasted_iota(jnp.int32, (pw, pw), 0)
    ci = lax.broadcasted_iota(jnp.int32, (pw, pw), 1)
    t_in = ri & (RWKV_HEAD - 1)
    s_in = ci & (RWKV_HEAD - 1)
    m_strict = jnp.where(t_in > s_in, 1.0, 0.0).astype(F32)
    m_incl = jnp.where(t_in >= s_in, 1.0, 0.0).astype(F32)
    eye = jnp.where(ri == ci, 1.0, 0.0).astype(F32)
    lane = lax.broadcasted_iota(jnp.int32, (c_len, pw), 1)
    lo_lane = lane < RWKV_HEAD
    tri = jnp.where(lax.broadcasted_iota(jnp.int32, (c_len, c_len), 0) >=
                    lax.broadcasted_iota(jnp.int32, (c_len, c_len), 1), 1.0, 0.0).astype(BF16)

    def stack(xp):
        return jnp.concatenate([jnp.where(lo_lane, xp, 0.0), jnp.where(lo_lane, 0.0, xp)], axis=0)

    def chunk_body(c, carry):
        rows = pl.ds(pl.multiple_of(c * c_len, c_len), c_len)
        lw_c = lw_s[rows, :]
        cum = _dot_exact_lhs(tri, lw_c)
        cum_prev = cum - lw_c
        c0 = cum[c_len // 2 - 1:c_len // 2, :]
        c_end = cum[c_len - 1:c_len, :]
        e_fwd = jnp.exp(cum - c0)
        e_fwd_prev = jnp.exp(cum_prev - c0)
        e_bwd = jnp.exp(c0 - cum)
        e_abs = jnp.exp(cum)
        e_abs_prev = jnp.exp(cum_prev)
        e_end = jnp.exp(c_end - cum)
        g_end = jnp.exp(c_end)
        r_c, k_c, v_c, z_c, b_c = r_s[rows, :], k_s[rows, :], v_s[rows, :], z_s[rows, :], b_s[rows, :]
        zt, rt = z_c * e_fwd_prev, r_c * e_fwd
        bt, kt = b_c * e_bwd, k_c * e_bwd
        zb, rb = z_c * e_abs_prev, r_c * e_abs
        bh, kh = b_c * e_end, k_c * e_end

        ps = [slice(p * pw, (p + 1) * pw) for p in range(n_pairs)]
        lm = [_bdot_nt(jnp.concatenate([stack(zt[:, s]), stack(rt[:, s])], axis=0),
                       jnp.concatenate([stack(bt[:, s]), stack(kt[:, s])], axis=0)) for s in ps]
        l_bz = [x_[0:pw, 0:pw] * m_strict for x_ in lm]
        l_kz = [x_[0:pw, pw:2 * pw] * m_strict for x_ in lm]
        m_rb = [x_[pw:2 * pw, 0:pw] * m_incl for x_ in lm]
        m_rk = [x_[pw:2 * pw, pw:2 * pw] * m_incl for x_ in lm]
        vs = [stack(v_c[:, s]) for s in ps]

        pwr = l_bz
        tm = [eye + x_ for x_ in l_bz]
        n_sq = int(math.log2(c_len)) - 1
        for _ in range(n_sq):
            pwr = [_bdot(x_, x_) for x_ in pwr]
            tm = [t_ + _bdot(t_, x_) for t_, x_ in zip(tm, pwr)]

        lv = [_bdot(l_, v_) for l_, v_ in zip(l_kz, vs)]
        xw = [_bdot(t_, jnp.concatenate([stack(zb[:, s]), lv_], axis=1)) for t_, s, lv_ in zip(tm, ps, lv)]
        bhs = [stack(bh[:, s]) for s in ps]
        gh = [_bdot_tn(b_, x_) for b_, x_ in zip(bhs, xw)]
        hk = [_bdot_tn(stack(kh[:, s]), v_) for s, v_ in zip(ps, vs)]
        qy = [_bdot(m_, x_) for m_, x_ in zip(m_rb, xw)]
        yk = [_bdot(m_, v_) for m_, v_ in zip(m_rk, vs)]
        for p in range(n_pairs):
            g_mat = eye * g_end[:, ps[p]] + gh[p][:, 0:pw]
            h_mat = gh[p][:, pw:2 * pw] + hk[p]
            q_mat = stack(rb[:, ps[p]]) + qy[p][:, 0:pw]
            y0 = qy[p][:, pw:2 * pw] + yk[p]
            st = st_ref[p]
            upd = _dot_split(jnp.concatenate([q_mat, g_mat], axis=0), st)
            y_st = upd[0:pw, :] + y0
            st_ref[p] = upd[pw:2 * pw, :] + h_mat
            y_s[rows, ps[p]] = y_st[0:c_len, :] + y_st[c_len:2 * c_len, :]
        return carry

    lax.fori_loop(0, tb // c_len, chunk_body, 0)

    y = y_s[...]
    inv_n = 1.0 / RWKV_HEAD
    mean = _bdot(y, seg) * inv_n
    dev = y - mean
    var = _bdot(dev * dev, seg) * inv_n
    yn = dev * lax.rsqrt(var + RWKV_LN_EPS) * lnw_ref[...] + lnb_ref[...]
    bonus = _bdot(r * k * rk_ref[...], seg) * v
    o_ref[...] = (yn + bonus) * g


def _rwkv(u_b, mu, w0, w2, a0, a2, g2, k_k, k_a, r_k, ln_w, ln_b, seg):
    t = u_b.shape[0]
    w = RWKV_WIDTH
    vec = pl.BlockSpec((1, w), lambda i: (0, 0))
    full = lambda arr: pl.BlockSpec(arr.shape, lambda i: (0, 0))
    blk = pltpu.VMEM((MIX_ROWS, w), F32)
    return pl.pallas_call(
        _rwkv_kernel,
        grid=(t // MIX_ROWS,),
        in_specs=[
            pl.BlockSpec((MIX_ROWS, RWKV_COLS), lambda i: (i, 0)),
            pl.BlockSpec((1, RWKV_COLS), lambda i: (0, 0)),
            vec, full(w2), vec, full(a2), full(g2), vec, vec, vec, vec, vec, full(seg),
        ],
        out_specs=pl.BlockSpec((MIX_ROWS, w), lambda i: (i, 0)),
        out_shape=jax.ShapeDtypeStruct((t, w), F32),
        scratch_shapes=[
            pltpu.VMEM((8, RWKV_COLS), F32),
            pltpu.VMEM((w // RWKV_PAIR, RWKV_PAIR, RWKV_PAIR), F32),
            blk, blk, blk, blk, blk, blk, blk,
        ],
        compiler_params=_params(1),
        name="rwkv7",
    )(u_b, mu, w0, w2, a0, a2, g2, k_k, k_a, r_k, ln_w, ln_b, seg)


def _ret_kernel(log_g, q_ref, k_ref, v_ref, g_ref, inv_ref, o_ref, st_ref):
    tb = MIX_ROWS

    @pl.when(pl.program_id(0) == 0)
    def _():
        st_ref[...] = jnp.zeros_like(st_ref)

    t_col = lax.broadcasted_iota(jnp.int32, (tb, 1), 0)
    pos = (pl.program_id(0) * tb + t_col).astype(F32)
    ang = pos * inv_ref[...]
    cos = jnp.cos(ang)
    sin = jnp.sin(ang)
    t_f = t_col.astype(F32)
    rel = (lax.broadcasted_iota(jnp.int32, (tb, tb), 0) - lax.broadcasted_iota(jnp.int32, (tb, tb), 1)).astype(F32)
    half = RET_DK // 2

    def rot(xh):
        x1, x2 = xh[:, :half], xh[:, half:]
        return jnp.concatenate([x1 * cos - x2 * sin, x2 * cos + x1 * sin], axis=1)

    for h in range(RET_HEADS):
        lg = log_g[h]
        qk_cols = slice(h * RET_DK, (h + 1) * RET_DK)
        v_cols = slice(h * RET_DV, (h + 1) * RET_DV)
        q = rot(q_ref[:, qk_cols])
        k = rot(k_ref[:, qk_cols]) * (RET_DK ** -0.5)
        v = v_ref[:, v_cols]
        intra = jnp.where(rel >= 0, jnp.exp(lg * jnp.maximum(rel, 0.0)), 0.0)
        s = _bdot_nt(q, k) * intra
        st = st_ref[h]
        y = _bdot(s, v) + _bdot(q, st) * jnp.exp(lg * (t_f + 1.0))
        st_ref[h] = st * math.exp(lg * tb) + _bdot_tn(k * jnp.exp(lg * (tb - 1.0 - t_f)), v)
        y = y * lax.rsqrt(jnp.mean(y * y, axis=-1, keepdims=True) + RET_EPS)
        o_ref[:, v_cols] = jax.nn.silu(g_ref[:, v_cols]) * y


def _retention(u, inv_freq):
    t = u.shape[0]
    qk_w = RET_HEADS * RET_DK
    v_w = RET_HEADS * RET_DV
    log_g = tuple(math.log1p(-(2.0 ** (-5.0 - h))) for h in range(RET_HEADS))
    return pl.pallas_call(
        functools.partial(_ret_kernel, log_g),
        grid=(t // MIX_ROWS,),
        in_specs=[
            pl.BlockSpec((MIX_ROWS, qk_w), lambda i: (i, 0)),
            pl.BlockSpec((MIX_ROWS, qk_w), lambda i: (i, 1)),
            pl.BlockSpec((MIX_ROWS, v_w), lambda i: (i, 1)),
            pl.BlockSpec((MIX_ROWS, v_w), lambda i: (i, 2)),
            pl.BlockSpec((1, RET_DK // 2), lambda i: (0, 0)),
        ],
        out_specs=pl.BlockSpec((MIX_ROWS, v_w), lambda i: (i, 0)),
        out_shape=jax.ShapeDtypeStruct((t, v_w), F32),
        scratch_shapes=[pltpu.VMEM((RET_HEADS, RET_DK, RET_DV), F32)],
        compiler_params=_params(1),
        name="retention",
    )(u, u, u, u, inv_freq)


def kernel(x, norm_mix_pre, norm_mix_post, norm_ffn_pre, norm_ffn_post, ffn_w_in, ffn_w_out, ab_w_in, ab_w_out, lru_conv_w, lru_conv_b, lru_wa, lru_ba, lru_wx, lru_bx, lru_lambda, rwkv_mu, rwkv_w0, rwkv_w2, rwkv_a0, rwkv_a2, rwkv_g2, rwkv_k_k, rwkv_k_a, rwkv_r_k, rwkv_ln_w, rwkv_ln_b, ret_w_in, ret_w_out):
    assert x.shape == (1, 16384, D_MODEL)
    depth = norm_mix_pre.shape[0]
    row = lambda vec: vec.reshape(1, -1)
    seg = jnp.kron(jnp.eye(RWKV_WIDTH // RWKV_HEAD, dtype=F32), jnp.ones((RWKV_HEAD, RWKV_HEAD), F32)).astype(BF16)
    half = RET_DK // 2
    inv_freq = (1.0 / (ROPE_BASE ** (jnp.arange(half, dtype=F32) / half))).reshape(1, half)

    h = x[0]
    for layer in range(depth):
        g_pre = row(norm_mix_pre[layer])
        g_post = row(norm_mix_post[layer])
        if layer % 2 == 0:
            e = layer // 2
            w_in = ab_w_in[e].astype(BF16)
            w_out = ab_w_out[e].astype(BF16)
            u_a = _norm_matmul(h, g_pre, w_in[:, :2 * LRU_WIDTH], 2 * LRU_WIDTH)
            u_b = _norm_matmul(h, g_pre, w_in[:, 2 * LRU_WIDTH:], RWKV_COLS)
            wa_bd = jax.scipy.linalg.block_diag(*lru_wa[e]).astype(BF16)
            wx_bd = jax.scipy.linalg.block_diag(*lru_wx[e]).astype(BF16)
            ya = _lru(u_a, lru_conv_w[e], row(lru_conv_b[e]), wa_bd, row(lru_ba[e]), wx_bd, row(lru_bx[e]),
                      row(lru_lambda[e]))
            yb = _rwkv(u_b, row(rwkv_mu[e]), row(rwkv_w0[e]), rwkv_w2[e].astype(BF16), row(rwkv_a0[e]),
                       rwkv_a2[e].astype(BF16), rwkv_g2[e].astype(BF16), row(rwkv_k_k[e]), row(rwkv_k_a[e]),
                       row(rwkv_r_k[e]), row(rwkv_ln_w[e]), row(rwkv_ln_b[e]), seg)
            h = _proj_norm_res([ya, yb], [w_out[:LRU_WIDTH], w_out[LRU_WIDTH:]], g_post, h)
        else:
            o = layer // 2
            u = _norm_matmul(h, g_pre, ret_w_in[o].astype(BF16), 2048)
            y = _retention(u, inv_freq)
            h = _proj_norm_res([y], [ret_w_out[o].astype(BF16)], g_post, h)
        h = _ffn(h, row(norm_ffn_pre[layer]), ffn_w_in[layer].astype(BF16), ffn_w_out[layer].astype(BF16),
                 row(norm_ffn_post[layer]))
    return h[None]
```

```python
import functools
import math

import jax
import jax.numpy as jnp
from jax import lax
from jax.experimental import pallas as pl
from jax.experimental.pallas import tpu as pltpu

F32 = jnp.float32
BF16 = jnp.bfloat16

D_MODEL = 1024
D_FF = 4 * D_MODEL
NORM_EPS = 1e-6

LRU_WIDTH = 512
LRU_BLOCKS = 8
LRU_C = 8.0

RWKV_WIDTH = 512
RWKV_HEAD = 64
RWKV_LN_EPS = 64e-5
RWKV_COLS = 3 * RWKV_WIDTH + 64 + 64 + 128
RWKV_CHUNK = 64
RWKV_PAIR = 2 * RWKV_HEAD

RET_HEADS = 4
RET_DK = 256
RET_DV = 512
RET_EPS = 1e-6
ROPE_BASE = 10000.0

MIX_ROWS = 256
MM_ROWS = 512
FF_CHUNK = 1024
VMEM_LIMIT = 56 * 1024 * 1024


def _bdot(a, b):
    return jnp.dot(a.astype(BF16), b.astype(BF16), preferred_element_type=F32)


def _bdot_nt(a, b):
    return lax.dot_general(a.astype(BF16), b.astype(BF16), (((1,), (1,)), ((), ())), preferred_element_type=F32)


def _bdot_tn(a, b):
    return lax.dot_general(a.astype(BF16), b.astype(BF16), (((0,), (0,)), ((), ())), preferred_element_type=F32)


def _split(x):
    hi = x.astype(BF16)
    lo = (x - hi.astype(F32)).astype(BF16)
    return hi, lo


def _dot_split(a, b):
    ah, al = _split(a)
    bh, bl = _split(b)
    d = lambda x, y: jnp.dot(x, y, preferred_element_type=F32)
    return d(ah, bh) + (d(ah, bl) + d(al, bh))


def _dot_exact_lhs(a_bf16, b):
    b0 = b.astype(BF16)
    r1 = b - b0.astype(F32)
    b1 = r1.astype(BF16)
    b2 = (r1 - b1.astype(F32)).astype(BF16)
    d = lambda y: jnp.dot(a_bf16, y, preferred_element_type=F32)
    return d(b0) + (d(b1) + d(b2))


def _rms(x, g):
    return x * lax.rsqrt(jnp.mean(x * x, axis=-1, keepdims=True) + NORM_EPS) * g


def _softplus(x):
    return jnp.maximum(x, 0.0) + jnp.log1p(jnp.exp(-jnp.abs(x)))


def _params(n_axes):
    return pltpu.CompilerParams(dimension_semantics=("arbitrary",) * n_axes, vmem_limit_bytes=VMEM_LIMIT)


def _norm_matmul_kernel(n_out, x_ref, g_ref, *refs):
    y = _rms(x_ref[...], g_ref[...]).astype(BF16)
    for w_ref, o_ref in zip(refs[:n_out], refs[n_out:]):
        o_ref[...] = jnp.dot(y, w_ref[...], preferred_element_type=F32).astype(o_ref.dtype)


def _norm_matmul(h, g, ws):
    t, d = h.shape
    in_specs = [pl.BlockSpec((MM_ROWS, d), lambda i: (i, 0)), pl.BlockSpec((1, d), lambda i: (0, 0))]
    in_specs += [pl.BlockSpec(w.shape, lambda i: (0, 0)) for w in ws]
    return pl.pallas_call(
        functools.partial(_norm_matmul_kernel, len(ws)),
        grid=(t // MM_ROWS,),
        in_specs=in_specs,
        out_specs=[pl.BlockSpec((MM_ROWS, w.shape[1]), lambda i: (i, 0)) for w in ws],
        out_shape=[jax.ShapeDtypeStruct((t, w.shape[1]), BF16) for w in ws],
        compiler_params=_params(1),
        name="norm_matmul",
    )(h, g, *ws)


def _proj_norm_res_kernel(n_in, *refs):
    y_refs = refs[:n_in]
    w_refs = refs[n_in:2 * n_in]
    g_ref, h_ref, o_ref = refs[2 * n_in:]
    m = jnp.dot(y_refs[0][...].astype(BF16), w_refs[0][...], preferred_element_type=F32)
    for y_ref, w_ref in zip(y_refs[1:], w_refs[1:]):
        m = m + jnp.dot(y_ref[...].astype(BF16), w_ref[...], preferred_element_type=F32)
    o_ref[...] = h_ref[...] + _rms(m, g_ref[...])


def _proj_norm_res(ys, ws, g, h):
    t, d = h.shape
    n_in = len(ys)
    in_specs = [pl.BlockSpec((MM_ROWS, y.shape[1]), lambda i: (i, 0)) for y in ys]
    in_specs += [pl.BlockSpec(w.shape, lambda i: (0, 0)) for w in ws]
    in_specs += [pl.BlockSpec((1, d), lambda i: (0, 0)), pl.BlockSpec((MM_ROWS, d), lambda i: (i, 0))]
    return pl.pallas_call(
        functools.partial(_proj_norm_res_kernel, n_in),
        grid=(t // MM_ROWS,),
        in_specs=in_specs,
        out_specs=pl.BlockSpec((MM_ROWS, d), lambda i: (i, 0)),
        out_shape=jax.ShapeDtypeStruct((t, d), F32),
        compiler_params=_params(1),
        name="proj_norm_res",
    )(*ys, *ws, g, h)


def _ffn_kernel(h_ref, gpre_ref, win_ref, wout_ref, gpost_ref, o_ref):
    h = h_ref[...]
    xn = _rms(h, gpre_ref[...]).astype(BF16)
    acc = jnp.zeros(h.shape, F32)
    for c in range(D_FF // FF_CHUNK):
        cols = slice(c * FF_CHUNK, (c + 1) * FF_CHUNK)
        a = jnp.dot(xn, win_ref[:, cols], preferred_element_type=F32)
        a = jnp.square(jnp.maximum(a, 0.0))
        acc = acc + jnp.dot(a.astype(BF16), wout_ref[cols, :], preferred_element_type=F32)
    o_ref[...] = h + _rms(acc, gpost_ref[...])


def _ffn(h, gpre, win, wout, gpost):
    t, d = h.shape
    return pl.pallas_call(
        _ffn_kernel,
        grid=(t // MM_ROWS,),
        in_specs=[
            pl.BlockSpec((MM_ROWS, d), lambda i: (i, 0)),
            pl.BlockSpec((1, d), lambda i: (0, 0)),
            pl.BlockSpec((d, D_FF), lambda i: (0, 0)),
            pl.BlockSpec((D_FF, d), lambda i: (0, 0)),
            pl.BlockSpec((1, d), lambda i: (0, 0)),
        ],
        out_specs=pl.BlockSpec((MM_ROWS, d), lambda i: (i, 0)),
        out_shape=jax.ShapeDtypeStruct((t, d), F32),
        compiler_params=_params(1),
        name="ffn",
    )(h, gpre, win, wout, gpost)


def _lru_kernel(ax_ref, ag_ref, cw_ref, cb_ref, wa_ref, ba_ref, wx_ref, bx_ref, lam_ref, o_ref, xc_ref, hc_ref):
    tb = MIX_ROWS

    @pl.when(pl.program_id(0) == 0)
    def _():
        xc_ref[...] = jnp.zeros_like(xc_ref)
        hc_ref[...] = jnp.zeros_like(hc_ref)

    x = ax_ref[...].astype(F32)
    row = lax.broadcasted_iota(jnp.int32, (tb, 1), 0)
    xc = xc_ref[...]
    cw = cw_ref[...]
    u = x * cw[3:4, :] + cb_ref[...]
    for d in (1, 2, 3):
        head = jnp.concatenate([pltpu.roll(xc, d, axis=0)] * (tb // 8), axis=0)
        shifted = jnp.where(row < d, head, pltpu.roll(x, d, axis=0))
        u = u + shifted * cw[3 - d:4 - d, :]
    xc_ref[...] = x[tb - 8:tb, :]

    r = jax.nn.sigmoid(_bdot(u, wa_ref[...]) + ba_ref[...])
    ig = jax.nn.sigmoid(_bdot(u, wx_ref[...]) + bx_ref[...])
    log_a = (-LRU_C) * r * _softplus(-lam_ref[...])
    a = jnp.exp(log_a)
    b = jnp.sqrt(-jnp.tanh(log_a) * (a * a + 1.0)) * (ig * u)

    d = 1
    while d < tb:
        if d < 8:
            keep = row >= d
            a_sh = jnp.where(keep, pltpu.roll(a, d, axis=0), 1.0)
            b_sh = jnp.where(keep, pltpu.roll(b, d, axis=0), 0.0)
        else:
            a_sh = jnp.concatenate([jnp.ones((d, a.shape[1]), F32), a[:tb - d]], axis=0)
            b_sh = jnp.concatenate([jnp.zeros((d, a.shape[1]), F32), b[:tb - d]], axis=0)
        b = a * b_sh + b
        a = a * a_sh
        d *= 2
    h = a * hc_ref[0:1, :] + b
    hc_ref[0:1, :] = h[tb - 1:tb, :]
    o_ref[...] = (h * jax.nn.gelu(ag_ref[...].astype(F32))).astype(o_ref.dtype)


def _lru(u_a, cw, cb, wa_bd, ba, wx_bd, bx, lam):
    t = u_a.shape[0]
    w = LRU_WIDTH
    vec = pl.BlockSpec((1, w), lambda i: (0, 0))
    mat = pl.BlockSpec((w, w), lambda i: (0, 0))
    return pl.pallas_call(
        _lru_kernel,
        grid=(t // MIX_ROWS,),
        in_specs=[
            pl.BlockSpec((MIX_ROWS, w), lambda i: (i, 0)),
            pl.BlockSpec((MIX_ROWS, w), lambda i: (i, 1)),
            pl.BlockSpec((4, w), lambda i: (0, 0)),
            vec, mat, vec, mat, vec, vec,
        ],
        out_specs=pl.BlockSpec((MIX_ROWS, w), lambda i: (i, 0)),
        out_shape=jax.ShapeDtypeStruct((t, w), BF16),
        scratch_shapes=[pltpu.VMEM((8, w), F32), pltpu.VMEM((8, w), F32)],
        compiler_params=_params(1),
        name="rg_lru",
    )(u_a, u_a, cw, cb, wa_bd, ba, wx_bd, bx, lam)


def _rwkv_kernel(bc_ref, mu_ref, w0_ref, w2_ref, a0_ref, a2_ref, g2_ref, kk_ref, ka_ref, rk_ref, lnw_ref, lnb_ref,
                 seg_ref, o_ref, carry_ref, st_ref):
    tb = MIX_ROWS
    c_len = RWKV_CHUNK
    pw = RWKV_PAIR
    n_pairs = RWKV_WIDTH // pw
    n_chunks = tb // c_len
    w = RWKV_WIDTH

    @pl.when(pl.program_id(0) == 0)
    def _():
        carry_ref[...] = jnp.zeros_like(carry_ref)
        st_ref[...] = jnp.zeros_like(st_ref)

    x = bc_ref[...].astype(F32)
    row = lax.broadcasted_iota(jnp.int32, (tb, 1), 0)
    prev = jnp.where(row == 0, carry_ref[0:1, :], pltpu.roll(x, 1, axis=0))
    carry_ref[0:1, :] = x[tb - 1:tb, :]
    xs = x + (prev - x) * mu_ref[...]

    seg = seg_ref[...]
    r = xs[:, 0:w]
    k = xs[:, w:2 * w]
    v = xs[:, 2 * w:3 * w]
    dw = xs[:, 3 * w:3 * w + 64]
    da = xs[:, 3 * w + 64:3 * w + 128]
    dg = xs[:, 3 * w + 128:3 * w + 256]
    w_log = -_softplus(-(w0_ref[...] + _bdot(jnp.tanh(dw), w2_ref[...]))) - 0.5
    lw = -jnp.exp(w_log)
    a = jax.nn.sigmoid(a0_ref[...] + _bdot(da, a2_ref[...]))
    g = _bdot(jax.nn.sigmoid(dg), g2_ref[...])
    kk = k * kk_ref[...]
    kk = kk / jnp.maximum(jnp.sqrt(_bdot(kk * kk, seg)), 1e-12)
    k = k * (1.0 + (a - 1.0) * ka_ref[...])
    z = -kk
    b = kk * a

    ri = lax.broadcasted_iota(jnp.int32, (pw, pw), 0)
    ci = lax.broadcasted_iota(jnp.int32, (pw, pw), 1)
    t_in = ri & (RWKV_HEAD - 1)
    s_in = ci & (RWKV_HEAD - 1)
    m_strict = jnp.where(t_in > s_in, 1.0, 0.0).astype(F32)
    m_incl = jnp.where(t_in >= s_in, 1.0, 0.0).astype(F32)
    eye = jnp.where(ri == ci, 1.0, 0.0).astype(F32)
    lo_lane = lax.broadcasted_iota(jnp.int32, (c_len, pw), 1) < RWKV_HEAD
    tri = jnp.where(lax.broadcasted_iota(jnp.int32, (c_len, c_len), 0) >=
                    lax.broadcasted_iota(jnp.int32, (c_len, c_len), 1), 1.0, 0.0).astype(BF16)
    zeros_pp = jnp.zeros((pw, pw), BF16)

    def stack(xp):
        return jnp.concatenate([jnp.where(lo_lane, xp, 0.0), jnp.where(lo_lane, 0.0, xp)], axis=0)

    def stack16(xp):
        return stack(xp).astype(BF16)

    cat0 = lambda *parts: jnp.concatenate(parts, axis=0)
    cat1 = lambda *parts: jnp.concatenate(parts, axis=1)

    zt, rt, bt, kt, zb, rb, bh, kh, vs, g_end = ([] for _ in range(10))
    for c in range(n_chunks):
        rows = slice(c * c_len, (c + 1) * c_len)
        lw_c = lw[rows]
        cum = _dot_exact_lhs(tri, lw_c)
        cum_prev = cum - lw_c
        c0 = cum[c_len // 2 - 1:c_len // 2, :]
        c_end = cum[c_len - 1:c_len, :]
        e_bwd = jnp.exp(c0 - cum)
        e_end = jnp.exp(c_end - cum)
        zt_c, rt_c = z[rows] * jnp.exp(cum_prev - c0), r[rows] * jnp.exp(cum - c0)
        bt_c, kt_c = b[rows] * e_bwd, k[rows] * e_bwd
        zb_c, rb_c = z[rows] * jnp.exp(cum_prev), r[rows] * jnp.exp(cum)
        bh_c, kh_c = b[rows] * e_end, k[rows] * e_end
        ge_c = jnp.exp(c_end)
        v_c = v[rows]
        for p in range(n_pairs):
            s = slice(p * pw, (p + 1) * pw)
            zt.append(stack16(zt_c[:, s]))
            rt.append(stack16(rt_c[:, s]))
            bt.append(stack16(bt_c[:, s]))
            kt.append(stack16(kt_c[:, s]))
            zb.append(stack16(zb_c[:, s]))
            rb.append(stack(rb_c[:, s]))
            bh.append(stack16(bh_c[:, s]))
            kh.append(stack16(kh_c[:, s]))
            vs.append(stack16(v_c[:, s]))
            g_end.append(ge_c[:, s])
    each = range(n_chunks * n_pairs)

    lm = [_bdot_nt(cat0(zt[i], rt[i]), cat0(bt[i], kt[i])) for i in each]
    l_bz = [lm[i][0:pw, 0:pw] * m_strict for i in each]
    l_kz = [lm[i][0:pw, pw:2 * pw] * m_strict for i in each]
    m_rbk = [cat1(lm[i][pw:2 * pw, 0:pw] * m_incl, lm[i][pw:2 * pw, pw:2 * pw] * m_incl).astype(BF16) for i in each]

    pwr = [_bdot(l_bz[i], l_bz[i]) for i in each]
    tm = [eye + l_bz[i] for i in each]
    for _ in range(int(math.log2(c_len)) - 2):
        res = [_bdot(pwr[i], cat1(pwr[i], tm[i])) for i in each]
        pwr = [res[i][:, 0:pw] for i in each]
        tm = [tm[i] + res[i][:, pw:2 * pw] for i in each]
    tm = [tm[i] + _bdot(pwr[i], tm[i]) for i in each]

    lv = [_bdot(l_kz[i], vs[i]) for i in each]
    xw = [_bdot(tm[i], cat1(zb[i], lv[i].astype(BF16))).astype(BF16) for i in each]
    rhs = [cat0(xw[i], cat1(zeros_pp, vs[i])) for i in each]
    gh = [_bdot_tn(cat0(bh[i], kh[i]), rhs[i]) for i in each]
    qy = [jnp.dot(m_rbk[i], rhs[i], preferred_element_type=F32) for i in each]

    st = [st_ref[p] for p in range(n_pairs)]
    y_rows = []
    for c in range(n_chunks):
        y_cols = []
        for p in range(n_pairs):
            i = c * n_pairs + p
            g_mat = eye * g_end[i] + gh[i][:, 0:pw]
            q_mat = rb[i] + qy[i][:, 0:pw]
            lhs_hi, lhs_lo = _split(cat0(q_mat, g_mat))
            st_hi, st_lo = _split(st[p])
            upd = jnp.dot(cat1(lhs_hi, lhs_hi, lhs_lo), cat0(st_hi, st_lo, st_hi), preferred_element_type=F32)
            y_st = upd[0:pw, :] + qy[i][:, pw:2 * pw]
            st[p] = upd[pw:2 * pw, :] + gh[i][:, pw:2 * pw]
            y_cols.append(y_st[0:c_len, :] + y_st[c_len:2 * c_len, :])
        y_rows.append(cat1(*y_cols))
    for p in range(n_pairs):
        st_ref[p] = st[p]
    y = cat0(*y_rows)

    inv_n = 1.0 / RWKV_HEAD
    mean = _bdot(y, seg) * inv_n
    dev = y - mean
    var = _bdot(dev * dev, seg) * inv_n
    yn = dev * lax.rsqrt(var + RWKV_LN_EPS) * lnw_ref[...] + lnb_ref[...]
    bonus = _bdot(r * k * rk_ref[...], seg) * v
    o_ref[...] = ((yn + bonus) * g).astype(o_ref.dtype)


def _rwkv(u_b, mu, w0, w2, a0, a2, g2, k_k, k_a, r_k, ln_w, ln_b, seg):
    t = u_b.shape[0]
    w = RWKV_WIDTH
    vec = pl.BlockSpec((1, w), lambda i: (0, 0))
    full = lambda arr: pl.BlockSpec(arr.shape, lambda i: (0, 0))
    return pl.pallas_call(
        _rwkv_kernel,
        grid=(t // MIX_ROWS,),
        in_specs=[
            pl.BlockSpec((MIX_ROWS, RWKV_COLS), lambda i: (i, 0)),
            pl.BlockSpec((1, RWKV_COLS), lambda i: (0, 0)),
            vec, full(w2), vec, full(a2), full(g2), vec, vec, vec, vec, vec, full(seg),
        ],
        out_specs=pl.BlockSpec((MIX_ROWS, w), lambda i: (i, 0)),
        out_shape=jax.ShapeDtypeStruct((t, w), BF16),
        scratch_shapes=[
            pltpu.VMEM((8, RWKV_COLS), F32),
            pltpu.VMEM((w // RWKV_PAIR, RWKV_PAIR, RWKV_PAIR), F32),
        ],
        compiler_params=_params(1),
        name="rwkv7",
    )(u_b, mu, w0, w2, a0, a2, g2, k_k, k_a, r_k, ln_w, ln_b, seg)


def _ret_kernel(log_g, q_ref, k_ref, v_ref, g_ref, inv_ref, o_ref, st_ref, cos_ref, sin_ref, intra_ref):
    tb = MIX_ROWS
    t_col = lax.broadcasted_iota(jnp.int32, (tb, 1), 0)
    t_f = t_col.astype(F32)

    @pl.when(pl.program_id(0) == 0)
    def _():
        st_ref[...] = jnp.zeros_like(st_ref)
        ang = t_f * inv_ref[...]
        cos_ref[...] = jnp.cos(ang)
        sin_ref[...] = jnp.sin(ang)
        rel = (lax.broadcasted_iota(jnp.int32, (tb, tb), 0) -
               lax.broadcasted_iota(jnp.int32, (tb, tb), 1)).astype(F32)
        for h in range(RET_HEADS):
            intra_ref[h] = jnp.where(rel >= 0, jnp.exp(log_g[h] * jnp.maximum(rel, 0.0)), 0.0)

    base = (pl.program_id(0) * tb).astype(F32) * inv_ref[...]
    cos_a, sin_a = jnp.cos(base), jnp.sin(base)
    cos_b, sin_b = cos_ref[...], sin_ref[...]
    cos = cos_a * cos_b - sin_a * sin_b
    sin = sin_a * cos_b + cos_a * sin_b
    half = RET_DK // 2

    def rot(xh):
        x1, x2 = xh[:, :half], xh[:, half:]
        return jnp.concatenate([x1 * cos - x2 * sin, x2 * cos + x1 * sin], axis=1)

    for h in range(RET_HEADS):
        lg = log_g[h]
        qk_cols = slice(h * RET_DK, (h + 1) * RET_DK)
        v_cols = slice(h * RET_DV, (h + 1) * RET_DV)
        q = rot(q_ref[:, qk_cols].astype(F32)).astype(BF16)
        k = rot(k_ref[:, qk_cols].astype(F32)) * (RET_DK ** -0.5)
        v = v_ref[:, v_cols]
        s = _bdot_nt(q, k) * intra_ref[h]
        st = st_ref[h]
        y = _bdot(s, v) + _bdot(q, st) * jnp.exp(lg * (t_f + 1.0))
        st_ref[h] = st * math.exp(lg * tb) + _bdot_tn(k * jnp.exp(lg * (tb - 1.0 - t_f)), v)
        y = y * lax.rsqrt(jnp.mean(y * y, axis=-1, keepdims=True) + RET_EPS)
        o_ref[:, v_cols] = (jax.nn.silu(g_ref[:, v_cols].astype(F32)) * y).astype(o_ref.dtype)


def _retention(u, inv_freq):
    t = u.shape[0]
    qk_w = RET_HEADS * RET_DK
    v_w = RET_HEADS * RET_DV
    log_g = tuple(math.log1p(-(2.0 ** (-5.0 - h))) for h in range(RET_HEADS))
    return pl.pallas_call(
        functools.partial(_ret_kernel, log_g),
        grid=(t // MIX_ROWS,),
        in_specs=[
            pl.BlockSpec((MIX_ROWS, qk_w), lambda i: (i, 0)),
            pl.BlockSpec((MIX_ROWS, qk_w), lambda i: (i, 1)),
            pl.BlockSpec((MIX_ROWS, v_w), lambda i: (i, 1)),
            pl.BlockSpec((MIX_ROWS, v_w), lambda i: (i, 2)),
            pl.BlockSpec((1, RET_DK // 2), lambda i: (0, 0)),
        ],
        out_specs=pl.BlockSpec((MIX_ROWS, v_w), lambda i: (i, 0)),
        out_shape=jax.ShapeDtypeStruct((t, v_w), BF16),
        scratch_shapes=[
            pltpu.VMEM((RET_HEADS, RET_DK, RET_DV), F32),
            pltpu.VMEM((MIX_ROWS, RET_DK // 2), F32),
            pltpu.VMEM((MIX_ROWS, RET_DK // 2), F32),
            pltpu.VMEM((RET_HEADS, MIX_ROWS, MIX_ROWS), F32),
        ],
        compiler_params=_params(1),
        name="retention",
    )(u, u, u, u, inv_freq)


def kernel(x, norm_mix_pre, norm_mix_post, norm_ffn_pre, norm_ffn_post, ffn_w_in, ffn_w_out, ab_w_in, ab_w_out, lru_conv_w, lru_conv_b, lru_wa, lru_ba, lru_wx, lru_bx, lru_lambda, rwkv_mu, rwkv_w0, rwkv_w2, rwkv_a0, rwkv_a2, rwkv_g2, rwkv_k_k, rwkv_k_a, rwkv_r_k, rwkv_ln_w, rwkv_ln_b, ret_w_in, ret_w_out):
    assert x.shape == (1, 16384, D_MODEL)
    depth = norm_mix_pre.shape[0]
    row = lambda vec: vec.reshape(1, -1)
    seg = jnp.kron(jnp.eye(RWKV_WIDTH // RWKV_HEAD, dtype=F32), jnp.ones((RWKV_HEAD, RWKV_HEAD), F32)).astype(BF16)
    half = RET_DK // 2
    inv_freq = (1.0 / (ROPE_BASE ** (jnp.arange(half, dtype=F32) / half))).reshape(1, half)

    h = x[0]
    for layer in range(depth):
        g_pre = row(norm_mix_pre[layer])
        g_post = row(norm_mix_post[layer])
        if layer % 2 == 0:
            e = layer // 2
            w_in = ab_w_in[e].astype(BF16)
            w_out = ab_w_out[e].astype(BF16)
            u_a, u_b = _norm_matmul(h, g_pre, [w_in[:, :2 * LRU_WIDTH], w_in[:, 2 * LRU_WIDTH:]])
            wa_bd = jax.scipy.linalg.block_diag(*lru_wa[e]).astype(BF16)
            wx_bd = jax.scipy.linalg.block_diag(*lru_wx[e]).astype(BF16)
            ya = _lru(u_a, lru_conv_w[e], row(lru_conv_b[e]), wa_bd, row(lru_ba[e]), wx_bd, row(lru_bx[e]),
                      row(lru_lambda[e]))
            yb = _rwkv(u_b, row(rwkv_mu[e]), row(rwkv_w0[e]), rwkv_w2[e].astype(BF16), row(rwkv_a0[e]),
                       rwkv_a2[e].astype(BF16), rwkv_g2[e].astype(BF16), row(rwkv_k_k[e]), row(rwkv_k_a[e]),
                       row(rwkv_r_k[e]), row(rwkv_ln_w[e]), row(rwkv_ln_b[e]), seg)
            h = _proj_norm_res([ya, yb], [w_out[:LRU_WIDTH], w_out[LRU_WIDTH:]], g_post, h)
        else:
            o = layer // 2
            (u,) = _norm_matmul(h, g_pre, [ret_w_in[o].astype(BF16)])
            y = _retention(u, inv_freq)
            h = _proj_norm_res([y], [ret_w_out[o].astype(BF16)], g_post, h)
        h = _ffn(h, row(norm_ffn_pre[layer]), ffn_w_in[layer].astype(BF16), ffn_w_out[layer].astype(BF16),
                 row(norm_ffn_post[layer]))
    return h[None]
```

```python
import functools
import math

import jax
import jax.numpy as jnp
from jax import lax
from jax.experimental import pallas as pl
from jax.experimental.pallas import tpu as pltpu

F32 = jnp.float32
BF16 = jnp.bfloat16

D_MODEL = 1024
D_FF = 4 * D_MODEL
NORM_EPS = 1e-6

LRU_WIDTH = 512
LRU_C = 8.0

RWKV_WIDTH = 512
RWKV_HEAD = 64
RWKV_LN_EPS = 64e-5
RWKV_COLS = 3 * RWKV_WIDTH + 64 + 64 + 128
RWKV_CHUNK = 64
RWKV_PAIR = 2 * RWKV_HEAD

RET_HEADS = 4
RET_DK = 256
RET_DV = 512
RET_EPS = 1e-6
ROPE_BASE = 10000.0

MIX_ROWS = 256
EVEN_ROWS = 256
MM_ROWS = 512
FF_CHUNK = 1024
VMEM_LIMIT = 56 * 1024 * 1024


def _bdot(a, b):
    return jnp.dot(a.astype(BF16), b.astype(BF16), preferred_element_type=F32)


def _bdot_nt(a, b):
    return lax.dot_general(a.astype(BF16), b.astype(BF16), (((1,), (1,)), ((), ())), preferred_element_type=F32)


def _bdot_tn(a, b):
    return lax.dot_general(a.astype(BF16), b.astype(BF16), (((0,), (0,)), ((), ())), preferred_element_type=F32)


def _split(x):
    hi = x.astype(BF16)
    lo = (x - hi.astype(F32)).astype(BF16)
    return hi, lo


def _dot_exact_lhs(a_bf16, b):
    b0 = b.astype(BF16)
    r1 = b - b0.astype(F32)
    b1 = r1.astype(BF16)
    b2 = (r1 - b1.astype(F32)).astype(BF16)
    d = lambda y: jnp.dot(a_bf16, y, preferred_element_type=F32)
    return d(b0) + (d(b1) + d(b2))


def _rms(x, g):
    return x * lax.rsqrt(jnp.mean(x * x, axis=-1, keepdims=True) + NORM_EPS) * g


def _softplus(x):
    return jnp.maximum(x, 0.0) + jnp.log1p(jnp.exp(-jnp.abs(x)))


def _params():
    return pltpu.CompilerParams(dimension_semantics=("arbitrary",), vmem_limit_bytes=VMEM_LIMIT)


def _const_spec(arr):
    return pl.BlockSpec(arr.shape, lambda i: (0,) * arr.ndim)


def _ffn_kernel(h_ref, gpre_ref, win_ref, wout_ref, gpost_ref, o_ref):
    h = h_ref[...]
    xn = _rms(h, gpre_ref[...]).astype(BF16)
    acc = jnp.zeros(h.shape, F32)
    for c in range(D_FF // FF_CHUNK):
        cols = slice(c * FF_CHUNK, (c + 1) * FF_CHUNK)
        a = jnp.dot(xn, win_ref[:, cols], preferred_element_type=F32)
        a = jnp.square(jnp.maximum(a, 0.0))
        acc = acc + jnp.dot(a.astype(BF16), wout_ref[cols, :], preferred_element_type=F32)
    o_ref[...] = h + _rms(acc, gpost_ref[...])


def _ffn(h, gpre, win, wout, gpost):
    t, d = h.shape
    blk = pl.BlockSpec((MM_ROWS, d), lambda i: (i, 0))
    return pl.pallas_call(
        _ffn_kernel,
        grid=(t // MM_ROWS,),
        in_specs=[blk, _const_spec(gpre), _const_spec(win), _const_spec(wout), _const_spec(gpost)],
        out_specs=blk,
        out_shape=jax.ShapeDtypeStruct((t, d), F32),
        compiler_params=_params(),
        name="ffn",
    )(h, gpre, win, wout, gpost)


def _lru_block(x, gate_in, cw_ref, cb_ref, wa_ref, ba_ref, wx_ref, bx_ref, lam_ref, xc_ref, hc_ref):
    tb = x.shape[0]
    row = lax.broadcasted_iota(jnp.int32, (tb, 1), 0)
    xc = xc_ref[...]
    cw = cw_ref[...]
    u = x * cw[3:4, :] + cb_ref[...]
    for d in (1, 2, 3):
        head = jnp.concatenate([pltpu.roll(xc, d, axis=0)] * (tb // 8), axis=0)
        shifted = jnp.where(row < d, head, pltpu.roll(x, d, axis=0))
        u = u + shifted * cw[3 - d:4 - d, :]
    xc_ref[...] = x[tb - 8:tb, :]

    r = jax.nn.sigmoid(_bdot(u, wa_ref[...]) + ba_ref[...])
    ig = jax.nn.sigmoid(_bdot(u, wx_ref[...]) + bx_ref[...])
    log_a = (-LRU_C) * r * _softplus(-lam_ref[...])
    a = jnp.exp(log_a)
    b = jnp.sqrt(-jnp.tanh(log_a) * (a * a + 1.0)) * (ig * u)

    d = 1
    while d < tb:
        if d < 8:
            keep = row >= d
            a_sh = jnp.where(keep, pltpu.roll(a, d, axis=0), 1.0)
            b_sh = jnp.where(keep, pltpu.roll(b, d, axis=0), 0.0)
        else:
            a_sh = jnp.concatenate([jnp.ones((d, a.shape[1]), F32), a[:tb - d]], axis=0)
            b_sh = jnp.concatenate([jnp.zeros((d, a.shape[1]), F32), b[:tb - d]], axis=0)
        b = a * b_sh + b
        a = a * a_sh
        d *= 2
    h = a * hc_ref[0:1, :] + b
    hc_ref[0:1, :] = h[tb - 1:tb, :]
    return h * jax.nn.gelu(gate_in)


def _rwkv_block(x, mu_ref, w0_ref, w2_ref, a0_ref, a2_ref, g2_ref, kk_ref, ka_ref, rk_ref, lnw_ref, lnb_ref,
                seg_ref, carry_ref, st_ref):
    tb = x.shape[0]
    c_len = RWKV_CHUNK
    pw = RWKV_PAIR
    n_pairs = RWKV_WIDTH // pw
    n_chunks = tb // c_len
    w = RWKV_WIDTH

    row = lax.broadcasted_iota(jnp.int32, (tb, 1), 0)
    prev = jnp.where(row == 0, carry_ref[0:1, :], pltpu.roll(x, 1, axis=0))
    carry_ref[0:1, :] = x[tb - 1:tb, :]
    xs = x + (prev - x) * mu_ref[...]

    seg = seg_ref[...]
    r = xs[:, 0:w]
    k = xs[:, w:2 * w]
    v = xs[:, 2 * w:3 * w]
    dw = xs[:, 3 * w:3 * w + 64]
    da = xs[:, 3 * w + 64:3 * w + 128]
    dg = xs[:, 3 * w + 128:3 * w + 256]
    w_log = -_softplus(-(w0_ref[...] + _bdot(jnp.tanh(dw), w2_ref[...]))) - 0.5
    lw = -jnp.exp(w_log)
    a = jax.nn.sigmoid(a0_ref[...] + _bdot(da, a2_ref[...]))
    g = _bdot(jax.nn.sigmoid(dg), g2_ref[...])
    kk = k * kk_ref[...]
    kk = kk / jnp.maximum(jnp.sqrt(_bdot(kk * kk, seg)), 1e-12)
    k = k * (1.0 + (a - 1.0) * ka_ref[...])
    z = -kk
    b = kk * a

    ri = lax.broadcasted_iota(jnp.int32, (pw, pw), 0)
    ci = lax.broadcasted_iota(jnp.int32, (pw, pw), 1)
    t_in = ri & (RWKV_HEAD - 1)
    s_in = ci & (RWKV_HEAD - 1)
    m_strict = jnp.where(t_in > s_in, 1.0, 0.0).astype(F32)
    m_incl = jnp.where(t_in >= s_in, 1.0, 0.0).astype(F32)
    eye = jnp.where(ri == ci, 1.0, 0.0).astype(F32)
    lo_lane = lax.broadcasted_iota(jnp.int32, (c_len, pw), 1) < RWKV_HEAD
    tri = jnp.where(lax.broadcasted_iota(jnp.int32, (c_len, c_len), 0) >=
                    lax.broadcasted_iota(jnp.int32, (c_len, c_len), 1), 1.0, 0.0).astype(BF16)
    zeros_pp = jnp.zeros((pw, pw), BF16)

    def stack(xp):
        return jnp.concatenate([jnp.where(lo_lane, xp, 0.0), jnp.where(lo_lane, 0.0, xp)], axis=0)

    def stack16(xp):
        return stack(xp).astype(BF16)

    cat0 = lambda *parts: jnp.concatenate(parts, axis=0)
    cat1 = lambda *parts: jnp.concatenate(parts, axis=1)

    zt, rt, bt, kt, zb, rb, bh, kh, vs, g_end = ([] for _ in range(10))
    for c in range(n_chunks):
        rows = slice(c * c_len, (c + 1) * c_len)
        lw_c = lw[rows]
        cum = _dot_exact_lhs(tri, lw_c)
        cum_prev = cum - lw_c
        c0 = cum[c_len // 2 - 1:c_len // 2, :]
        c_end = cum[c_len - 1:c_len, :]
        e_bwd = jnp.exp(c0 - cum)
        e_end = jnp.exp(c_end - cum)
        zt_c, rt_c = z[rows] * jnp.exp(cum_prev - c0), r[rows] * jnp.exp(cum - c0)
        bt_c, kt_c = b[rows] * e_bwd, k[rows] * e_bwd
        zb_c, rb_c = z[rows] * jnp.exp(cum_prev), r[rows] * jnp.exp(cum)
        bh_c, kh_c = b[rows] * e_end, k[rows] * e_end
        ge_c = jnp.exp(c_end)
        v_c = v[rows]
        for p in range(n_pairs):
            s = slice(p * pw, (p + 1) * pw)
            zt.append(stack16(zt_c[:, s]))
            rt.append(stack16(rt_c[:, s]))
            bt.append(stack16(bt_c[:, s]))
            kt.append(stack16(kt_c[:, s]))
            zb.append(stack16(zb_c[:, s]))
            rb.append(stack(rb_c[:, s]))
            bh.append(stack16(bh_c[:, s]))
            kh.append(stack16(kh_c[:, s]))
            vs.append(stack16(v_c[:, s]))
            g_end.append(ge_c[:, s])
    each = range(n_chunks * n_pairs)

    lm = [_bdot_nt(cat0(zt[i], rt[i]), cat0(bt[i], kt[i])) for i in each]
    l_bz = [lm[i][0:pw, 0:pw] * m_strict for i in each]
    l_kz = [lm[i][0:pw, pw:2 * pw] * m_strict for i in each]
    m_rbk = [cat1(lm[i][pw:2 * pw, 0:pw] * m_incl, lm[i][pw:2 * pw, pw:2 * pw] * m_incl).astype(BF16) for i in each]

    pwr = [_bdot(l_bz[i], l_bz[i]) for i in each]
    tm = [eye + l_bz[i] for i in each]
    for _ in range(int(math.log2(c_len)) - 2):
        res = [_bdot(pwr[i], cat1(pwr[i], tm[i])) for i in each]
        pwr = [res[i][:, 0:pw] for i in each]
        tm = [tm[i] + res[i][:, pw:2 * pw] for i in each]
    tm = [tm[i] + _bdot(pwr[i], tm[i]) for i in each]

    lv = [_bdot(l_kz[i], vs[i]) for i in each]
    xw = [_bdot(tm[i], cat1(zb[i], lv[i].astype(BF16))).astype(BF16) for i in each]
    rhs = [cat0(xw[i], cat1(zeros_pp, vs[i])) for i in each]
    gh = [_bdot_tn(cat0(bh[i], kh[i]), rhs[i]) for i in each]
    qy = [jnp.dot(m_rbk[i], rhs[i], preferred_element_type=F32) for i in each]

    st = [st_ref[p] for p in range(n_pairs)]
    y_rows = []
    for c in range(n_chunks):
        y_cols = []
        for p in range(n_pairs):
            i = c * n_pairs + p
            g_mat = eye * g_end[i] + gh[i][:, 0:pw]
            q_mat = rb[i] + qy[i][:, 0:pw]
            lhs = cat0(q_mat, g_mat).astype(BF16)
            st_hi, st_lo = _split(st[p])
            upd = jnp.dot(cat1(lhs, lhs), cat0(st_hi, st_lo), preferred_element_type=F32)
            y_st = upd[0:pw, :] + qy[i][:, pw:2 * pw]
            st[p] = upd[pw:2 * pw, :] + gh[i][:, pw:2 * pw]
            y_cols.append(y_st[0:c_len, :] + y_st[c_len:2 * c_len, :])
        y_rows.append(cat1(*y_cols))
    for p in range(n_pairs):
        st_ref[p] = st[p]
    y = cat0(*y_rows)

    inv_n = 1.0 / RWKV_HEAD
    mean = _bdot(y, seg) * inv_n
    dev = y - mean
    var = _bdot(dev * dev, seg) * inv_n
    yn = dev * lax.rsqrt(var + RWKV_LN_EPS) * lnw_ref[...] + lnb_ref[...]
    bonus = _bdot(r * k * rk_ref[...], seg) * v
    return (yn + bonus) * g


N_LRU_REFS = 7
N_RWKV_REFS = 12


def _even_kernel(h_ref, gpre_ref, wa_ref, wb_ref, *refs):
    lru_refs = refs[:N_LRU_REFS]
    rwkv_refs = refs[N_LRU_REFS:N_LRU_REFS + N_RWKV_REFS]
    woa_ref, wob_ref, gpost_ref, o_ref, xc_ref, hc_ref, carry_ref, st_ref = refs[N_LRU_REFS + N_RWKV_REFS:]

    @pl.when(pl.program_id(0) == 0)
    def _():
        for ref in (xc_ref, hc_ref, carry_ref, st_ref):
            ref[...] = jnp.zeros_like(ref)

    h = h_ref[...]
    hn = _rms(h, gpre_ref[...]).astype(BF16)
    u_a = jnp.dot(hn, wa_ref[...], preferred_element_type=F32)
    u_b = jnp.dot(hn, wb_ref[...], preferred_element_type=F32)
    ya = _lru_block(u_a[:, :LRU_WIDTH], u_a[:, LRU_WIDTH:], *lru_refs, xc_ref, hc_ref)
    yb = _rwkv_block(u_b, *rwkv_refs, carry_ref, st_ref)
    m = _bdot(ya, woa_ref[...]) + _bdot(yb, wob_ref[...])
    o_ref[...] = h + _rms(m, gpost_ref[...])


def _even_mixer(h, gpre, wa_in, wb_in, lru_args, rwkv_args, wo_a, wo_b, gpost):
    t, d = h.shape
    blk = pl.BlockSpec((EVEN_ROWS, d), lambda i: (i, 0))
    consts = [gpre, wa_in, wb_in, *lru_args, *rwkv_args, wo_a, wo_b, gpost]
    return pl.pallas_call(
        _even_kernel,
        grid=(t // EVEN_ROWS,),
        in_specs=[blk] + [_const_spec(c) for c in consts],
        out_specs=blk,
        out_shape=jax.ShapeDtypeStruct((t, d), F32),
        scratch_shapes=[
            pltpu.VMEM((8, LRU_WIDTH), F32),
            pltpu.VMEM((8, LRU_WIDTH), F32),
            pltpu.VMEM((8, RWKV_COLS), F32),
            pltpu.VMEM((RWKV_WIDTH // RWKV_PAIR, RWKV_PAIR, RWKV_PAIR), F32),
        ],
        compiler_params=_params(),
        name="lru_rwkv_mixer",
    )(h, *consts)


def _odd_kernel(log_g, h_ref, gpre_ref, win_ref, inv_ref, wout_ref, gpost_ref, o_ref, st_ref, cos_ref, sin_ref,
                intra_ref):
    tb = MIX_ROWS
    t_col = lax.broadcasted_iota(jnp.int32, (tb, 1), 0)
    t_f = t_col.astype(F32)

    @pl.when(pl.program_id(0) == 0)
    def _():
        st_ref[...] = jnp.zeros_like(st_ref)
        ang = t_f * inv_ref[...]
        cos_ref[...] = jnp.cos(ang)
        sin_ref[...] = jnp.sin(ang)
        rel = (lax.broadcasted_iota(jnp.int32, (tb, tb), 0) -
               lax.broadcasted_iota(jnp.int32, (tb, tb), 1)).astype(F32)
        for h in range(RET_HEADS):
            intra_ref[h] = jnp.where(rel >= 0, jnp.exp(log_g[h] * jnp.maximum(rel, 0.0)), 0.0)

    base = (pl.program_id(0) * tb).astype(F32) * inv_ref[...]
    cos_a, sin_a = jnp.cos(base), jnp.sin(base)
    cos_b, sin_b = cos_ref[...], sin_ref[...]
    cos = cos_a * cos_b - sin_a * sin_b
    sin = sin_a * cos_b + cos_a * sin_b
    half = RET_DK // 2
    qk_w = RET_HEADS * RET_DK
    v_w = RET_HEADS * RET_DV

    def rot(xh):
        x1, x2 = xh[:, :half], xh[:, half:]
        return jnp.concatenate([x1 * cos - x2 * sin, x2 * cos + x1 * sin], axis=1)

    x = h_ref[...]
    hn = _rms(x, gpre_ref[...]).astype(BF16)
    proj = lambda start, width: jnp.dot(hn, win_ref[:, start:start + width], preferred_element_type=F32)

    m = jnp.zeros(x.shape, F32)
    for h in range(RET_HEADS):
        lg = log_g[h]
        q = rot(proj(h * RET_DK, RET_DK)).astype(BF16)
        k = rot(proj(qk_w + h * RET_DK, RET_DK)) * (RET_DK ** -0.5)
        v = proj(2 * qk_w + h * RET_DV, RET_DV).astype(BF16)
        gate = proj(2 * qk_w + v_w + h * RET_DV, RET_DV)
        s = _bdot_nt(q, k) * intra_ref[h]
        st = st_ref[h]
        y = _bdot(s, v) + _bdot(q, st) * jnp.exp(lg * (t_f + 1.0))
        st_ref[h] = st * math.exp(lg * tb) + _bdot_tn(k * jnp.exp(lg * (tb - 1.0 - t_f)), v)
        y = y * lax.rsqrt(jnp.mean(y * y, axis=-1, keepdims=True) + RET_EPS)
        m = m + _bdot(jax.nn.silu(gate) * y, wout_ref[h * RET_DV:(h + 1) * RET_DV, :])
    o_ref[...] = x + _rms(m, gpost_ref[...])


def _odd_mixer(h, gpre, win, inv_freq, wout, gpost):
    t, d = h.shape
    blk = pl.BlockSpec((MIX_ROWS, d), lambda i: (i, 0))
    log_g = tuple(math.log1p(-(2.0 ** (-5.0 - hd))) for hd in range(RET_HEADS))
    consts = [gpre, win, inv_freq, wout, gpost]
    return pl.pallas_call(
        functools.partial(_odd_kernel, log_g),
        grid=(t // MIX_ROWS,),
        in_specs=[blk] + [_const_spec(c) for c in consts],
        out_specs=blk,
        out_shape=jax.ShapeDtypeStruct((t, d), F32),
        scratch_shapes=[
            pltpu.VMEM((RET_HEADS, RET_DK, RET_DV), F32),
            pltpu.VMEM((MIX_ROWS, RET_DK // 2), F32),
            pltpu.VMEM((MIX_ROWS, RET_DK // 2), F32),
            pltpu.VMEM((RET_HEADS, MIX_ROWS, MIX_ROWS), F32),
        ],
        compiler_params=_params(),
        name="retention_mixer",
    )(h, *consts)


def kernel(x, norm_mix_pre, norm_mix_post, norm_ffn_pre, norm_ffn_post, ffn_w_in, ffn_w_out, ab_w_in, ab_w_out, lru_conv_w, lru_conv_b, lru_wa, lru_ba, lru_wx, lru_bx, lru_lambda, rwkv_mu, rwkv_w0, rwkv_w2, rwkv_a0, rwkv_a2, rwkv_g2, rwkv_k_k, rwkv_k_a, rwkv_r_k, rwkv_ln_w, rwkv_ln_b, ret_w_in, ret_w_out):
    assert x.shape == (1, 16384, D_MODEL)
    depth = norm_mix_pre.shape[0]
    row = lambda vec: vec.reshape(1, -1)
    seg = jnp.kron(jnp.eye(RWKV_WIDTH // RWKV_HEAD, dtype=F32), jnp.ones((RWKV_HEAD, RWKV_HEAD), F32)).astype(BF16)
    half = RET_DK // 2
    inv_freq = (1.0 / (ROPE_BASE ** (jnp.arange(half, dtype=F32) / half))).reshape(1, half)

    h = x[0]
    for layer in range(depth):
        g_pre = row(norm_mix_pre[layer])
        g_post = row(norm_mix_post[layer])
        if layer % 2 == 0:
            e = layer // 2
            w_in = ab_w_in[e].astype(BF16)
            w_out = ab_w_out[e].astype(BF16)
            wa_bd = jax.scipy.linalg.block_diag(*lru_wa[e]).astype(BF16)
            wx_bd = jax.scipy.linalg.block_diag(*lru_wx[e]).astype(BF16)
            lru_args = [lru_conv_w[e], row(lru_conv_b[e]), wa_bd, row(lru_ba[e]), wx_bd, row(lru_bx[e]),
                        row(lru_lambda[e])]
            rwkv_args = [row(rwkv_mu[e]), row(rwkv_w0[e]), rwkv_w2[e].astype(BF16), row(rwkv_a0[e]),
                         rwkv_a2[e].astype(BF16), rwkv_g2[e].astype(BF16), row(rwkv_k_k[e]), row(rwkv_k_a[e]),
                         row(rwkv_r_k[e]), row(rwkv_ln_w[e]), row(rwkv_ln_b[e]), seg]
            assert len(lru_args) == N_LRU_REFS and len(rwkv_args) == N_RWKV_REFS
            h = _even_mixer(h, g_pre, w_in[:, :2 * LRU_WIDTH], w_in[:, 2 * LRU_WIDTH:], lru_args, rwkv_args,
                            w_out[:LRU_WIDTH], w_out[LRU_WIDTH:], g_post)
        else:
            o = layer // 2
            h = _odd_mixer(h, g_pre, ret_w_in[o].astype(BF16), inv_freq, ret_w_out[o].astype(BF16), g_post)
        h = _ffn(h, row(norm_ffn_pre[layer]), ffn_w_in[layer].astype(BF16), ffn_w_out[layer].astype(BF16),
                 row(norm_ffn_post[layer]))
    return h[None]
```

```python
import functools
import math

import jax
import jax.numpy as jnp
import numpy as np
from jax import lax
from jax.experimental import pallas as pl
from jax.experimental.pallas import tpu as pltpu

F32 = jnp.float32
BF16 = jnp.bfloat16

D_MODEL = 1024
D_FF = 4 * D_MODEL
NORM_EPS = 1e-6

LRU_WIDTH = 512
LRU_C = 8.0

RWKV_WIDTH = 512
RWKV_HEAD = 64
RWKV_LN_EPS = 64e-5
KK_NORM_FLOOR = 1e-12
RWKV_COLS = 3 * RWKV_WIDTH + 64 + 64 + 128
RWKV_CHUNK = 64
RWKV_PAIR = 2 * RWKV_HEAD
SEG_WIDTH = 256

RET_HEADS = 4
RET_DK = 256
RET_DV = 512
RET_EPS = 1e-6
ROPE_BASE = 10000.0

MIX_ROWS = 256
MM_ROWS = 512
FF_CHUNK = 1024
VMEM_LIMIT = 56 * 1024 * 1024


def _bdot(a, b):
    return jnp.dot(a.astype(BF16), b.astype(BF16), preferred_element_type=F32)


def _bdot_nt(a, b):
    return lax.dot_general(a.astype(BF16), b.astype(BF16), (((1,), (1,)), ((), ())), preferred_element_type=F32)


def _bdot_tn(a, b):
    return lax.dot_general(a.astype(BF16), b.astype(BF16), (((0,), (0,)), ((), ())), preferred_element_type=F32)


def _split(x):
    hi = x.astype(BF16)
    lo = (x - hi.astype(F32)).astype(BF16)
    return hi, lo


def _dot_exact_lhs(a_bf16, b):
    b0 = b.astype(BF16)
    r1 = b - b0.astype(F32)
    b1 = r1.astype(BF16)
    b2 = (r1 - b1.astype(F32)).astype(BF16)
    d = lambda y: jnp.dot(a_bf16, y, preferred_element_type=F32)
    return d(b0) + (d(b1) + d(b2))


def _rms(x, g):
    return x * lax.rsqrt(jnp.mean(x * x, axis=-1, keepdims=True) + NORM_EPS) * g


def _softplus(x):
    return jnp.maximum(x, 0.0) + jnp.log(1.0 + jnp.exp(-jnp.abs(x)))


def _params():
    return pltpu.CompilerParams(dimension_semantics=("arbitrary",), vmem_limit_bytes=VMEM_LIMIT)


def _const_spec(arr):
    return pl.BlockSpec(arr.shape, lambda i: (0,) * arr.ndim)


def _layer_spec(stack, layer):
    return pl.BlockSpec((None,) + stack.shape[1:], lambda i: (layer, 0, 0))


def _ffn_kernel(h_ref, gpre_ref, win_ref, wout_ref, gpost_ref, o_ref):
    h = h_ref[...]
    xn = _rms(h, gpre_ref[...]).astype(BF16)
    acc = jnp.zeros(h.shape, F32)
    for c in range(D_FF // FF_CHUNK):
        cols = slice(c * FF_CHUNK, (c + 1) * FF_CHUNK)
        a = jnp.dot(xn, win_ref[:, cols], preferred_element_type=F32)
        a = jnp.square(jnp.maximum(a, 0.0))
        acc = acc + jnp.dot(a.astype(BF16), wout_ref[cols, :], preferred_element_type=F32)
    o_ref[...] = h + _rms(acc, gpost_ref[...])


def _ffn(h, gpre, win, wout, gpost, layer):
    t, d = h.shape
    blk = pl.BlockSpec((MM_ROWS, d), lambda i: (i, 0))
    return pl.pallas_call(
        _ffn_kernel,
        grid=(t // MM_ROWS,),
        in_specs=[blk, _const_spec(gpre), _layer_spec(win, layer), _layer_spec(wout, layer), _const_spec(gpost)],
        out_specs=blk,
        out_shape=jax.ShapeDtypeStruct((t, d), F32),
        compiler_params=_params(),
        name="ffn",
    )(h, gpre, win, wout, gpost)


def _lru_block(x, gate_in, cw_ref, cb_ref, wa_ref, ba_ref, wx_ref, bx_ref, lam_ref, xc_ref, hc_ref):
    tb = x.shape[0]
    row = lax.broadcasted_iota(jnp.int32, (tb, 1), 0)
    xc = xc_ref[...]
    cw = cw_ref[...]
    u = x * cw[3:4, :] + cb_ref[...]
    for d in (1, 2, 3):
        head = jnp.concatenate([pltpu.roll(xc, d, axis=0)] * (tb // 8), axis=0)
        shifted = jnp.where(row < d, head, pltpu.roll(x, d, axis=0))
        u = u + shifted * cw[3 - d:4 - d, :]
    xc_ref[...] = x[tb - 8:tb, :]

    r = jax.nn.sigmoid(_bdot(u, wa_ref[...]) + ba_ref[...])
    ig = jax.nn.sigmoid(_bdot(u, wx_ref[...]) + bx_ref[...])
    log_a = (-LRU_C) * r * _softplus(-lam_ref[...])
    a = jnp.exp(log_a)
    b = jnp.sqrt(-jnp.tanh(log_a) * (a * a + 1.0)) * (ig * u)

    d = 1
    while d < tb:
        if d < 8:
            keep = row >= d
            a_sh = jnp.where(keep, pltpu.roll(a, d, axis=0), 1.0)
            b_sh = jnp.where(keep, pltpu.roll(b, d, axis=0), 0.0)
        else:
            a_sh = jnp.concatenate([jnp.ones((d, a.shape[1]), F32), a[:tb - d]], axis=0)
            b_sh = jnp.concatenate([jnp.zeros((d, a.shape[1]), F32), b[:tb - d]], axis=0)
        b = a * b_sh + b
        a = a * a_sh
        d *= 2
    h = a * hc_ref[0:1, :] + b
    hc_ref[0:1, :] = h[tb - 1:tb, :]
    return h * jax.nn.gelu(gate_in)


def _head_sums(x, seg_ref):
    seg = seg_ref[...]
    parts = [_bdot(x[:, c:c + SEG_WIDTH], seg) for c in range(0, x.shape[1], SEG_WIDTH)]
    return jnp.concatenate(parts, axis=1)


def _rwkv_block(x, mu_ref, w0_ref, w2_ref, a0_ref, a2_ref, g2_ref, kk_ref, ka_ref, rk_ref, lnw_ref, lnb_ref,
                seg_ref, masks_ref, carry_ref, st_ref):
    tb = x.shape[0]
    c_len = RWKV_CHUNK
    pw = RWKV_PAIR
    n_pairs = RWKV_WIDTH // pw
    n_chunks = tb // c_len
    w = RWKV_WIDTH

    row = lax.broadcasted_iota(jnp.int32, (tb, 1), 0)
    prev = jnp.where(row == 0, carry_ref[0:1, :], pltpu.roll(x, 1, axis=0))
    carry_ref[0:1, :] = x[tb - 1:tb, :]
    xs = x + (prev - x) * mu_ref[...]

    r = xs[:, 0:w]
    k = xs[:, w:2 * w]
    v = xs[:, 2 * w:3 * w]
    dw = xs[:, 3 * w:3 * w + 64]
    da = xs[:, 3 * w + 64:3 * w + 128]
    dg = xs[:, 3 * w + 128:3 * w + 256]
    w_log = -_softplus(-(w0_ref[...] + _bdot(jnp.tanh(dw), w2_ref[...]))) - 0.5
    lw = -jnp.exp(w_log)
    a = jax.nn.sigmoid(a0_ref[...] + _bdot(da, a2_ref[...]))
    g = _bdot(jax.nn.sigmoid(dg), g2_ref[...])
    kk = k * kk_ref[...]
    kk = kk * lax.rsqrt(jnp.maximum(_head_sums(kk * kk, seg_ref), KK_NORM_FLOOR * KK_NORM_FLOOR))
    k = k * (1.0 + (a - 1.0) * ka_ref[...])
    z = -kk
    b = kk * a

    m_strict, m_incl, eye = masks_ref[0], masks_ref[1], masks_ref[2]
    lo_lane = lax.broadcasted_iota(jnp.int32, (c_len, pw), 1) < RWKV_HEAD
    tri = jnp.where(lax.broadcasted_iota(jnp.int32, (c_len, c_len), 0) >=
                    lax.broadcasted_iota(jnp.int32, (c_len, c_len), 1), 1.0, 0.0).astype(BF16)
    zeros_pp = jnp.zeros((pw, pw), BF16)

    def stack(xp):
        return jnp.concatenate([jnp.where(lo_lane, xp, 0.0), jnp.where(lo_lane, 0.0, xp)], axis=0)

    def stack16(xp):
        return stack(xp).astype(BF16)

    cat0 = lambda *parts: jnp.concatenate(parts, axis=0)
    cat1 = lambda *parts: jnp.concatenate(parts, axis=1)

    zt, rt, bt, kt, zb, rb, bh, kh, vs, g_end = ([] for _ in range(10))
    for c in range(n_chunks):
        rows = slice(c * c_len, (c + 1) * c_len)
        lw_c = lw[rows]
        cum = _dot_exact_lhs(tri, lw_c)
        cum_prev = cum - lw_c
        c0 = cum[c_len // 2 - 1:c_len // 2, :]
        c_end = cum[c_len - 1:c_len, :]
        e_bwd = jnp.exp(c0 - cum)
        e_end = jnp.exp(c_end - cum)
        zt_c, rt_c = z[rows] * jnp.exp(cum_prev - c0), r[rows] * jnp.exp(cum - c0)
        bt_c, kt_c = b[rows] * e_bwd, k[rows] * e_bwd
        zb_c, rb_c = z[rows] * jnp.exp(cum_prev), r[rows] * jnp.exp(cum)
        bh_c, kh_c = b[rows] * e_end, k[rows] * e_end
        ge_c = jnp.exp(c_end)
        v_c = v[rows]
        for p in range(n_pairs):
            s = slice(p * pw, (p + 1) * pw)
            zt.append(stack16(zt_c[:, s]))
            rt.append(stack16(rt_c[:, s]))
            bt.append(stack16(bt_c[:, s]))
            kt.append(stack16(kt_c[:, s]))
            zb.append(stack16(zb_c[:, s]))
            rb.append(stack(rb_c[:, s]))
            bh.append(stack16(bh_c[:, s]))
            kh.append(stack16(kh_c[:, s]))
            vs.append(stack16(v_c[:, s]))
            g_end.append(ge_c[:, s])
    each = range(n_chunks * n_pairs)

    lm = [_bdot_nt(cat0(zt[i], rt[i]), cat0(bt[i], kt[i])) for i in each]
    yield None
    l_bz = [lm[i][0:pw, 0:pw] * m_strict for i in each]
    l_kz = [lm[i][0:pw, pw:2 * pw] * m_strict for i in each]
    m_rbk = [cat1(lm[i][pw:2 * pw, 0:pw] * m_incl, lm[i][pw:2 * pw, pw:2 * pw] * m_incl).astype(BF16) for i in each]

    pwr = [_bdot(l_bz[i], l_bz[i]) for i in each]
    tm = [eye + l_bz[i] for i in each]
    for _ in range(int(math.log2(c_len)) - 2):
        res = [_bdot(pwr[i], cat1(pwr[i], tm[i])) for i in each]
        pwr = [res[i][:, 0:pw] for i in each]
        tm = [tm[i] + res[i][:, pw:2 * pw] for i in each]
    tm = [tm[i] + _bdot(pwr[i], tm[i]) for i in each]

    lv = [_bdot(l_kz[i], vs[i]) for i in each]
    xw = [_bdot(tm[i], cat1(zb[i], lv[i].astype(BF16))).astype(BF16) for i in each]
    rhs = [cat0(xw[i], cat1(zeros_pp, vs[i])) for i in each]
    gh = [_bdot_tn(cat0(bh[i], kh[i]), rhs[i]) for i in each]
    qy = [jnp.dot(m_rbk[i], rhs[i], preferred_element_type=F32) for i in each]

    st = [st_ref[p] for p in range(n_pairs)]
    y_rows = []
    for c in range(n_chunks):
        y_cols = []
        for p in range(n_pairs):
            i = c * n_pairs + p
            g_mat = eye * g_end[i] + gh[i][:, 0:pw]
            q_mat = rb[i] + qy[i][:, 0:pw]
            lhs = cat0(q_mat, g_mat).astype(BF16)
            st_hi, st_lo = _split(st[p])
            upd = jnp.dot(cat1(lhs, lhs), cat0(st_hi, st_lo), preferred_element_type=F32)
            y_st = upd[0:pw, :] + qy[i][:, pw:2 * pw]
            st[p] = upd[pw:2 * pw, :] + gh[i][:, pw:2 * pw]
            y_cols.append(y_st[0:c_len, :] + y_st[c_len:2 * c_len, :])
        y_rows.append(cat1(*y_cols))
    for p in range(n_pairs):
        st_ref[p] = st[p]
    y = cat0(*y_rows)

    inv_n = 1.0 / RWKV_HEAD
    mean = _head_sums(y, seg_ref) * inv_n
    dev = y - mean
    var = _head_sums(dev * dev, seg_ref) * inv_n
    yn = dev * lax.rsqrt(var + RWKV_LN_EPS) * lnw_ref[...] + lnb_ref[...]
    bonus = _head_sums(r * k * rk_ref[...], seg_ref) * v
    yield (yn + bonus) * g


N_LRU_REFS = 7
N_RWKV_REFS = 13


def _pair_masks():
    t = np.arange(RWKV_PAIR) % RWKV_HEAD
    strict = t[:, None] > t[None, :]
    incl = t[:, None] >= t[None, :]
    return jnp.asarray(np.stack([strict, incl, np.eye(RWKV_PAIR, dtype=bool)]).astype(np.float32))


def _even_kernel(h_ref, gpre_ref, win_ref, *refs):
    lru_refs = refs[:N_LRU_REFS]
    rwkv_refs = refs[N_LRU_REFS:N_LRU_REFS + N_RWKV_REFS]
    wout_ref, gpost_ref, o_ref, xc_ref, hc_ref, carry_ref, st_ref = refs[N_LRU_REFS + N_RWKV_REFS:]

    @pl.when(pl.program_id(0) == 0)
    def _():
        for ref in (xc_ref, hc_ref, carry_ref, st_ref):
            ref[...] = jnp.zeros_like(ref)

    h = h_ref[...]
    hn = _rms(h, gpre_ref[...]).astype(BF16)
    u_b = jnp.dot(hn, win_ref[:, 2 * LRU_WIDTH:], preferred_element_type=F32)
    u_a = jnp.dot(hn, win_ref[:, :2 * LRU_WIDTH], preferred_element_type=F32)
    rwkv = _rwkv_block(u_b, *rwkv_refs, carry_ref, st_ref)
    next(rwkv)
    ya = _lru_block(u_a[:, :LRU_WIDTH], u_a[:, LRU_WIDTH:], *lru_refs, xc_ref, hc_ref)
    yb = next(rwkv)
    m = _bdot(ya, wout_ref[:LRU_WIDTH, :]) + _bdot(yb, wout_ref[LRU_WIDTH:, :])
    o_ref[...] = h + _rms(m, gpost_ref[...])


def _even_mixer(h, gpre, win, lru_args, rwkv_args, wout, gpost, layer):
    t, d = h.shape
    blk = pl.BlockSpec((MIX_ROWS, d), lambda i: (i, 0))
    consts = [gpre, win, *lru_args, *rwkv_args, wout, gpost]
    small = [*lru_args, *rwkv_args]
    return pl.pallas_call(
        _even_kernel,
        grid=(t // MIX_ROWS,),
        in_specs=([blk, _const_spec(gpre), _layer_spec(win, layer)] + [_const_spec(c) for c in small] +
                  [_layer_spec(wout, layer), _const_spec(gpost)]),
        out_specs=blk,
        out_shape=jax.ShapeDtypeStruct((t, d), F32),
        scratch_shapes=[
            pltpu.VMEM((8, LRU_WIDTH), F32),
            pltpu.VMEM((8, LRU_WIDTH), F32),
            pltpu.VMEM((8, RWKV_COLS), F32),
            pltpu.VMEM((RWKV_WIDTH // RWKV_PAIR, RWKV_PAIR, RWKV_PAIR), F32),
        ],
        compiler_params=_params(),
        name="lru_rwkv_mixer",
    )(h, *consts)


def _odd_kernel(log_g, h_ref, gpre_ref, win_ref, inv_ref, wout_ref, gpost_ref, o_ref, st_ref, cos_ref, sin_ref,
                intra_ref):
    tb = MIX_ROWS
    t_col = lax.broadcasted_iota(jnp.int32, (tb, 1), 0)
    t_f = t_col.astype(F32)

    @pl.when(pl.program_id(0) == 0)
    def _():
        st_ref[...] = jnp.zeros_like(st_ref)
        ang = t_f * inv_ref[...]
        cos_ref[...] = jnp.cos(ang)
        sin_ref[...] = jnp.sin(ang)
        rel = (lax.broadcasted_iota(jnp.int32, (tb, tb), 0) -
               lax.broadcasted_iota(jnp.int32, (tb, tb), 1)).astype(F32)
        for h in range(RET_HEADS):
            intra_ref[h] = jnp.where(rel >= 0, jnp.exp(log_g[h] * jnp.maximum(rel, 0.0)), 0.0)

    base = (pl.program_id(0) * tb).astype(F32) * inv_ref[...]
    cos_a, sin_a = jnp.cos(base), jnp.sin(base)
    cos_b, sin_b = cos_ref[...], sin_ref[...]
    cos = cos_a * cos_b - sin_a * sin_b
    sin = sin_a * cos_b + cos_a * sin_b
    half = RET_DK // 2
    qk_w = RET_HEADS * RET_DK
    v_w = RET_HEADS * RET_DV

    def rot(xh):
        x1, x2 = xh[:, :half], xh[:, half:]
        return jnp.concatenate([x1 * cos - x2 * sin, x2 * cos + x1 * sin], axis=1)

    x = h_ref[...]
    hn = _rms(x, gpre_ref[...]).astype(BF16)
    proj = lambda start, width: jnp.dot(hn, win_ref[:, start:start + width], preferred_element_type=F32)

    m = jnp.zeros(x.shape, F32)
    for h in range(RET_HEADS):
        lg = log_g[h]
        q = rot(proj(h * RET_DK, RET_DK)).astype(BF16)
        k = rot(proj(qk_w + h * RET_DK, RET_DK)) * (RET_DK ** -0.5)
        v = proj(2 * qk_w + h * RET_DV, RET_DV).astype(BF16)
        gate = proj(2 * qk_w + v_w + h * RET_DV, RET_DV)
        s = _bdot_nt(q, k) * intra_ref[h]
        st = st_ref[h]
        y = _bdot(s, v) + _bdot(q, st) * jnp.exp(lg * (t_f + 1.0))
        st_ref[h] = st * math.exp(lg * tb) + _bdot_tn(k * jnp.exp(lg * (tb - 1.0 - t_f)), v)
        y = y * lax.rsqrt(jnp.mean(y * y, axis=-1, keepdims=True) + RET_EPS)
        m = m + _bdot(jax.nn.silu(gate) * y, wout_ref[h * RET_DV:(h + 1) * RET_DV, :])
    o_ref[...] = x + _rms(m, gpost_ref[...])


def _odd_mixer(h, gpre, win, inv_freq, wout, gpost, layer):
    t, d = h.shape
    blk = pl.BlockSpec((MIX_ROWS, d), lambda i: (i, 0))
    log_g = tuple(math.log1p(-(2.0 ** (-5.0 - hd))) for hd in range(RET_HEADS))
    consts = [gpre, win, inv_freq, wout, gpost]
    return pl.pallas_call(
        functools.partial(_odd_kernel, log_g),
        grid=(t // MIX_ROWS,),
        in_specs=[blk, _const_spec(gpre), _layer_spec(win, layer), _const_spec(inv_freq), _layer_spec(wout, layer),
                  _const_spec(gpost)],
        out_specs=blk,
        out_shape=jax.ShapeDtypeStruct((t, d), F32),
        scratch_shapes=[
            pltpu.VMEM((RET_HEADS, RET_DK, RET_DV), F32),
            pltpu.VMEM((MIX_ROWS, RET_DK // 2), F32),
            pltpu.VMEM((MIX_ROWS, RET_DK // 2), F32),
            pltpu.VMEM((RET_HEADS, MIX_ROWS, MIX_ROWS), F32),
        ],
        compiler_params=_params(),
        name="retention_mixer",
    )(h, *consts)


def kernel(x, norm_mix_pre, norm_mix_post, norm_ffn_pre, norm_ffn_post, ffn_w_in, ffn_w_out, ab_w_in, ab_w_out, lru_conv_w, lru_conv_b, lru_wa, lru_ba, lru_wx, lru_bx, lru_lambda, rwkv_mu, rwkv_w0, rwkv_w2, rwkv_a0, rwkv_a2, rwkv_g2, rwkv_k_k, rwkv_k_a, rwkv_r_k, rwkv_ln_w, rwkv_ln_b, ret_w_in, ret_w_out):
    assert x.shape == (1, 16384, D_MODEL)
    depth = norm_mix_pre.shape[0]
    row = lambda vec: vec.reshape(1, -1)
    seg = jnp.kron(jnp.eye(SEG_WIDTH // RWKV_HEAD, dtype=F32), jnp.ones((RWKV_HEAD, RWKV_HEAD), F32)).astype(BF16)
    masks = _pair_masks()
    half = RET_DK // 2
    inv_freq = (1.0 / (ROPE_BASE ** (jnp.arange(half, dtype=F32) / half))).reshape(1, half)
    ffn_w_in = ffn_w_in.astype(BF16)
    ffn_w_out = ffn_w_out.astype(BF16)
    ab_w_in = ab_w_in.astype(BF16)
    ab_w_out = ab_w_out.astype(BF16)
    ret_w_in = ret_w_in.astype(BF16)
    ret_w_out = ret_w_out.astype(BF16)

    h = x[0]
    for layer in range(depth):
        g_pre = row(norm_mix_pre[layer])
        g_post = row(norm_mix_post[layer])
        if layer % 2 == 0:
            e = layer // 2
            wa_bd = jax.scipy.linalg.block_diag(*lru_wa[e]).astype(BF16)
            wx_bd = jax.scipy.linalg.block_diag(*lru_wx[e]).astype(BF16)
            lru_args = [lru_conv_w[e], row(lru_conv_b[e]), wa_bd, row(lru_ba[e]), wx_bd, row(lru_bx[e]),
                        row(lru_lambda[e])]
            rwkv_args = [row(rwkv_mu[e]), row(rwkv_w0[e]), rwkv_w2[e].astype(BF16), row(rwkv_a0[e]),
                         rwkv_a2[e].astype(BF16), rwkv_g2[e].astype(BF16), row(rwkv_k_k[e]), row(rwkv_k_a[e]),
                         row(rwkv_r_k[e]), row(rwkv_ln_w[e]), row(rwkv_ln_b[e]), seg, masks]
            assert len(lru_args) == N_LRU_REFS and len(rwkv_args) == N_RWKV_REFS
            h = _even_mixer(h, g_pre, ab_w_in, lru_args, rwkv_args, ab_w_out, g_post, e)
        else:
            h = _odd_mixer(h, g_pre, ret_w_in, inv_freq, ret_w_out, g_post, layer // 2)
        h = _ffn(h, row(norm_ffn_pre[layer]), ffn_w_in, ffn_w_out, row(norm_ffn_post[layer]), layer)
    return h[None]
```

```python
import functools
import math

import jax
import jax.numpy as jnp
import numpy as np
from jax import lax
from jax.experimental import pallas as pl
from jax.experimental.pallas import tpu as pltpu

F32 = jnp.float32
BF16 = jnp.bfloat16

D_MODEL = 1024
D_FF = 4 * D_MODEL
NORM_EPS = 1e-6

LRU_WIDTH = 512
LRU_C = 8.0

RWKV_WIDTH = 512
RWKV_HEAD = 64
RWKV_LN_EPS = 64e-5
KK_NORM_FLOOR = 1e-12
RWKV_COLS = 3 * RWKV_WIDTH + 64 + 64 + 128
RWKV_CHUNK = 64
RWKV_PAIR = 2 * RWKV_HEAD
SEG_WIDTH = 256

RET_HEADS = 4
RET_DK = 256
RET_DV = 512
RET_EPS = 1e-6
ROPE_BASE = 10000.0

MIX_ROWS = 256
RET_CHUNK = 256
ODD_ROWS = 512
MM_ROWS = 512
FF_CHUNK = 1024
VMEM_LIMIT = 56 * 1024 * 1024


def _bdot(a, b):
    return jnp.dot(a.astype(BF16), b.astype(BF16), preferred_element_type=F32)


def _bdot_nt(a, b):
    return lax.dot_general(a.astype(BF16), b.astype(BF16), (((1,), (1,)), ((), ())), preferred_element_type=F32)


def _bdot_tn(a, b):
    return lax.dot_general(a.astype(BF16), b.astype(BF16), (((0,), (0,)), ((), ())), preferred_element_type=F32)


def _split(x):
    hi = x.astype(BF16)
    lo = (x - hi.astype(F32)).astype(BF16)
    return hi, lo


def _dot_exact_lhs(a_bf16, b):
    b0 = b.astype(BF16)
    r1 = b - b0.astype(F32)
    b1 = r1.astype(BF16)
    b2 = (r1 - b1.astype(F32)).astype(BF16)
    d = lambda y: jnp.dot(a_bf16, y, preferred_element_type=F32)
    return d(b0) + (d(b1) + d(b2))


def _rms(x, g):
    return x * lax.rsqrt(jnp.mean(x * x, axis=-1, keepdims=True) + NORM_EPS) * g


def _softplus(x):
    return jnp.maximum(x, 0.0) + jnp.log(1.0 + jnp.exp(-jnp.abs(x)))


def _params():
    return pltpu.CompilerParams(dimension_semantics=("arbitrary",), vmem_limit_bytes=VMEM_LIMIT)


def _const_spec(arr):
    return pl.BlockSpec(arr.shape, lambda i: (0,) * arr.ndim)


def _layer_spec(stack, layer):
    return pl.BlockSpec((None,) + stack.shape[1:], lambda i: (layer, 0, 0))


def _ffn_kernel(h_ref, gpre_ref, win_ref, wout_ref, gpost_ref, o_ref):
    h = h_ref[...]
    xn = _rms(h, gpre_ref[...]).astype(BF16)
    acc = jnp.zeros(h.shape, F32)
    for c in range(D_FF // FF_CHUNK):
        cols = slice(c * FF_CHUNK, (c + 1) * FF_CHUNK)
        a = jnp.dot(xn, win_ref[:, cols], preferred_element_type=F32)
        a = jnp.square(jnp.maximum(a, 0.0))
        acc = acc + jnp.dot(a.astype(BF16), wout_ref[cols, :], preferred_element_type=F32)
    o_ref[...] = h + _rms(acc, gpost_ref[...])


def _ffn(h, gpre, win, wout, gpost, layer):
    t, d = h.shape
    blk = pl.BlockSpec((MM_ROWS, d), lambda i: (i, 0))
    return pl.pallas_call(
        _ffn_kernel,
        grid=(t // MM_ROWS,),
        in_specs=[blk, _const_spec(gpre), _layer_spec(win, layer), _layer_spec(wout, layer), _const_spec(gpost)],
        out_specs=blk,
        out_shape=jax.ShapeDtypeStruct((t, d), F32),
        compiler_params=_params(),
        name="ffn",
    )(h, gpre, win, wout, gpost)


def _lru_block(x, gate_in, cw_ref, cb_ref, wa_ref, ba_ref, wx_ref, bx_ref, lam_ref, xc_ref, hc_ref, placeholder):
    tb = x.shape[0]
    row = lax.broadcasted_iota(jnp.int32, (tb, 1), 0)
    xc = xc_ref[...]
    cw = cw_ref[...]
    u = x * cw[3:4, :] + cb_ref[...]
    for d in (1, 2, 3):
        head = jnp.concatenate([pltpu.roll(xc, d, axis=0)] * (tb // 8), axis=0)
        shifted = jnp.where(row < d, head, pltpu.roll(x, d, axis=0))
        u = u + shifted * cw[3 - d:4 - d, :]
    xc_ref[...] = x[tb - 8:tb, :]

    r = jax.nn.sigmoid(_bdot(u, wa_ref[...]) + ba_ref[...])
    ig = jax.nn.sigmoid(_bdot(u, wx_ref[...]) + bx_ref[...])
    log_a = (-LRU_C) * r * _softplus(-lam_ref[...])
    a = jnp.exp(log_a)
    b = jnp.sqrt(-jnp.tanh(log_a) * (a * a + 1.0)) * (ig * u)

    d = 1
    while d < tb:
        if d < 8:
            keep = row >= d
            a_sh = jnp.where(keep, pltpu.roll(a, d, axis=0), 1.0)
            b_sh = jnp.where(keep, pltpu.roll(b, d, axis=0), 0.0)
        else:
            a_sh = jnp.concatenate([jnp.ones((d, a.shape[1]), F32), a[:tb - d]], axis=0)
            b_sh = jnp.concatenate([jnp.zeros((d, a.shape[1]), F32), b[:tb - d]], axis=0)
        b = a * b_sh + b
        a = a * a_sh
        d *= 2
    h = a * hc_ref[0:1, :] + b
    hc_ref[0:1, :] = jnp.where(placeholder, 0.0, h[tb - 1:tb, :])
    return h * jax.nn.gelu(gate_in)


def _head_sums(x, seg_ref):
    seg = seg_ref[...]
    parts = [_bdot(x[:, c:c + SEG_WIDTH], seg) for c in range(0, x.shape[1], SEG_WIDTH)]
    return jnp.concatenate(parts, axis=1)


def _rwkv_block(x, mu_ref, w0_ref, w2_ref, a0_ref, a2_ref, g2_ref, kk_ref, ka_ref, rk_ref, lnw_ref, lnb_ref,
                seg_ref, masks_ref, carry_ref, st_ref, fill):
    tb = x.shape[0]
    c_len = RWKV_CHUNK
    pw = RWKV_PAIR
    n_pairs = RWKV_WIDTH // pw
    n_chunks = tb // c_len
    w = RWKV_WIDTH

    row = lax.broadcasted_iota(jnp.int32, (tb, 1), 0)
    prev = jnp.where(row == 0, carry_ref[0:1, :], pltpu.roll(x, 1, axis=0))
    carry_ref[0:1, :] = x[tb - 1:tb, :]
    xs = x + (prev - x) * mu_ref[...]

    r = xs[:, 0:w]
    k = xs[:, w:2 * w]
    v = xs[:, 2 * w:3 * w]
    dw = xs[:, 3 * w:3 * w + 64]
    da = xs[:, 3 * w + 64:3 * w + 128]
    dg = xs[:, 3 * w + 128:3 * w + 256]
    fill()
    w_log = -_softplus(-(w0_ref[...] + _bdot(jnp.tanh(dw), w2_ref[...]))) - 0.5
    lw = -jnp.exp(w_log)
    fill()
    a = jax.nn.sigmoid(a0_ref[...] + _bdot(da, a2_ref[...]))
    g = _bdot(jax.nn.sigmoid(dg), g2_ref[...])
    fill()
    kk = k * kk_ref[...]
    kk = kk * lax.rsqrt(jnp.maximum(_head_sums(kk * kk, seg_ref), KK_NORM_FLOOR * KK_NORM_FLOOR))
    k = k * (1.0 + (a - 1.0) * ka_ref[...])
    z = -kk
    b = kk * a

    m_strict, m_incl, eye = masks_ref[0], masks_ref[1], masks_ref[2]
    lo_lane = lax.broadcasted_iota(jnp.int32, (c_len, pw), 1) < RWKV_HEAD
    tri = jnp.where(lax.broadcasted_iota(jnp.int32, (c_len, c_len), 0) >=
                    lax.broadcasted_iota(jnp.int32, (c_len, c_len), 1), 1.0, 0.0).astype(BF16)
    zeros_pp = jnp.zeros((pw, pw), BF16)

    def stack(xp):
        return jnp.concatenate([jnp.where(lo_lane, xp, 0.0), jnp.where(lo_lane, 0.0, xp)], axis=0)

    def stack16(xp):
        xb = xp.astype(BF16)
        zero = jnp.zeros_like(xb)
        return jnp.concatenate([jnp.where(lo_lane, xb, zero), jnp.where(lo_lane, zero, xb)], axis=0)

    cat0 = lambda *parts: jnp.concatenate(parts, axis=0)
    cat1 = lambda *parts: jnp.concatenate(parts, axis=1)

    zt, rt, bt, kt, zb, rb, bh, kh, vs, g_end = ([] for _ in range(10))
    for c in range(n_chunks):
        rows = slice(c * c_len, (c + 1) * c_len)
        lw_c = lw[rows]
        fill()
        cum = _dot_exact_lhs(tri, lw_c)
        cum_prev = cum - lw_c
        c0 = cum[c_len // 2 - 1:c_len // 2, :]
        c_end = cum[c_len - 1:c_len, :]
        e_abs = jnp.exp(cum)
        e_abs_prev = jnp.exp(cum_prev)
        inv_abs = 1.0 / e_abs
        inv_c0 = jnp.exp(-c0)
        ge_c = jnp.exp(c_end)
        e_bwd = jnp.exp(c0) * inv_abs
        e_end = ge_c * inv_abs
        zt_c, rt_c = z[rows] * (e_abs_prev * inv_c0), r[rows] * (e_abs * inv_c0)
        bt_c, kt_c = b[rows] * e_bwd, k[rows] * e_bwd
        zb_c, rb_c = z[rows] * e_abs_prev, r[rows] * e_abs
        bh_c, kh_c = b[rows] * e_end, k[rows] * e_end
        v_c = v[rows]
        for p in range(n_pairs):
            s = slice(p * pw, (p + 1) * pw)
            zt.append(stack16(zt_c[:, s]))
            rt.append(stack16(rt_c[:, s]))
            bt.append(stack16(bt_c[:, s]))
            kt.append(stack16(kt_c[:, s]))
            zb.append(stack16(zb_c[:, s]))
            rb.append(stack(rb_c[:, s]))
            bh.append(stack16(bh_c[:, s]))
            kh.append(stack16(kh_c[:, s]))
            vs.append(stack16(v_c[:, s]))
            g_end.append(ge_c[:, s])
    each = range(n_chunks * n_pairs)

    lm = []
    for i in each:
        if i % n_pairs == 0:
            fill()
        lm.append(_bdot_nt(cat0(zt[i], rt[i]), cat0(bt[i], kt[i])))
    l_bz = [lm[i][0:pw, 0:pw] * m_strict for i in each]
    l_kz = [lm[i][0:pw, pw:2 * pw] * m_strict for i in each]
    m_rbk = [cat1(lm[i][pw:2 * pw, 0:pw] * m_incl, lm[i][pw:2 * pw, pw:2 * pw] * m_incl).astype(BF16) for i in each]

    pwr = [_bdot(l_bz[i], l_bz[i]) for i in each]
    tm = [eye + l_bz[i] for i in each]
    for it in range(int(math.log2(c_len)) - 2):
        res = [_bdot(pwr[i], cat1(pwr[i], tm[i])) for i in each]
        pwr = [res[i][:, 0:pw] for i in each]
        tm = [tm[i] + res[i][:, pw:2 * pw] for i in each]
        if it == 0:
            yield None
    tm = [tm[i] + _bdot(pwr[i], tm[i]) for i in each]

    lv = [_bdot(l_kz[i], vs[i]) for i in each]
    xw = [_bdot(tm[i], cat1(zb[i], lv[i].astype(BF16))).astype(BF16) for i in each]
    rhs = [cat0(xw[i], cat1(zeros_pp, vs[i])) for i in each]
    gh = [_bdot_tn(cat0(bh[i], kh[i]), rhs[i]) for i in each]
    qy = [jnp.dot(m_rbk[i], rhs[i], preferred_element_type=F32) for i in each]

    st = [st_ref[p] for p in range(n_pairs)]
    y_rows = []
    for c in range(n_chunks):
        y_cols = []
        for p in range(n_pairs):
            i = c * n_pairs + p
            g_mat = eye * g_end[i] + gh[i][:, 0:pw]
            q_mat = rb[i] + qy[i][:, 0:pw]
            lhs = cat0(q_mat, g_mat).astype(BF16)
            st_hi, st_lo = _split(st[p])
            upd = jnp.dot(cat1(lhs, lhs), cat0(st_hi, st_lo), preferred_element_type=F32)
            y_st = upd[0:pw, :] + qy[i][:, pw:2 * pw]
            st[p] = upd[pw:2 * pw, :] + gh[i][:, pw:2 * pw]
            y_cols.append(y_st[0:c_len, :] + y_st[c_len:2 * c_len, :])
        y_rows.append(cat1(*y_cols))
    for p in range(n_pairs):
        st_ref[p] = st[p]
    y = cat0(*y_rows)

    inv_n = 1.0 / RWKV_HEAD
    mean = _head_sums(y, seg_ref) * inv_n
    dev = y - mean
    var = _head_sums(dev * dev, seg_ref) * inv_n
    yn = dev * lax.rsqrt(var + RWKV_LN_EPS) * lnw_ref[...] + lnb_ref[...]
    bonus = _head_sums(r * k * rk_ref[...], seg_ref) * v
    yield (yn + bonus) * g


N_LRU_REFS = 7
N_RWKV_REFS = 13


def _pair_masks():
    t = np.arange(RWKV_PAIR) % RWKV_HEAD
    strict = t[:, None] > t[None, :]
    incl = t[:, None] >= t[None, :]
    return jnp.asarray(np.stack([strict, incl, np.eye(RWKV_PAIR, dtype=bool)]).astype(np.float32))


PROJ_PIECE = 256


def _in_proj_pieces(h_ref, gpre_ref, win_ref, u_ref):
    hn = _rms(h_ref[...], gpre_ref[...]).astype(BF16)
    for c in range(0, u_ref.shape[1], PROJ_PIECE):
        u_ref[:, c:c + PROJ_PIECE] = jnp.dot(hn, win_ref[:, c:c + PROJ_PIECE], preferred_element_type=F32)
        yield


def _even_kernel(h_next_ref, h_ref, gpre_ref, win_ref, *refs):
    lru_refs = refs[:N_LRU_REFS]
    rwkv_refs = refs[N_LRU_REFS:N_LRU_REFS + N_RWKV_REFS]
    wout_ref, gpost_ref, o_ref, xc_ref, hc_ref, carry_ref, st_ref, u0_ref, u1_ref = refs[N_LRU_REFS + N_RWKV_REFS:]
    step = pl.program_id(0)

    @pl.when(step == 0)
    def _():
        for ref in (xc_ref, hc_ref, carry_ref, st_ref, u0_ref, u1_ref):
            ref[...] = jnp.zeros_like(ref)

    def run(cur_ref, nxt_ref):
        proj = _in_proj_pieces(h_next_ref, gpre_ref, win_ref, nxt_ref)
        fill = lambda: next(proj, None)
        u = cur_ref[...]
        rwkv = _rwkv_block(u[:, 2 * LRU_WIDTH:], *rwkv_refs, carry_ref, st_ref, fill)
        next(rwkv)
        ya = _lru_block(u[:, :LRU_WIDTH], u[:, LRU_WIDTH:2 * LRU_WIDTH], *lru_refs, xc_ref, hc_ref, step == 0)
        yb = next(rwkv)
        for _ in proj:
            pass
        m = _bdot(ya, wout_ref[:LRU_WIDTH, :]) + _bdot(yb, wout_ref[LRU_WIDTH:, :])
        o_ref[...] = h_ref[...] + _rms(m, gpost_ref[...])

    parity = lax.rem(step, 2)

    @pl.when(parity == 0)
    def _():
        run(u1_ref, u0_ref)

    @pl.when(parity == 1)
    def _():
        run(u0_ref, u1_ref)


def _even_mixer(h, gpre, win, lru_args, rwkv_args, wout, gpost, layer):
    t, d = h.shape
    n_blocks = t // MIX_ROWS
    nxt = pl.BlockSpec((MIX_ROWS, d), lambda i: (jnp.minimum(i, n_blocks - 1), 0))
    cur = pl.BlockSpec((MIX_ROWS, d), lambda i: (jnp.maximum(i - 1, 0), 0))
    small = [*lru_args, *rwkv_args]
    u_buf = pltpu.VMEM((MIX_ROWS, win.shape[2]), F32)
    return pl.pallas_call(
        _even_kernel,
        grid=(n_blocks + 1,),
        in_specs=([nxt, cur, _const_spec(gpre), _layer_spec(win, layer)] + [_const_spec(c) for c in small] +
                  [_layer_spec(wout, layer), _const_spec(gpost)]),
        out_specs=cur,
        out_shape=jax.ShapeDtypeStruct((t, d), F32),
        scratch_shapes=[
            pltpu.VMEM((8, LRU_WIDTH), F32),
            pltpu.VMEM((8, LRU_WIDTH), F32),
            pltpu.VMEM((8, RWKV_COLS), F32),
            pltpu.VMEM((RWKV_WIDTH // RWKV_PAIR, RWKV_PAIR, RWKV_PAIR), F32),
            u_buf, u_buf,
        ],
        compiler_params=_params(),
        name="lru_rwkv_mixer",
    )(h, h, gpre, win, *small, wout, gpost)


def _odd_kernel(log_g, h_ref, gpre_ref, win_ref, inv_ref, wout_ref, gpost_ref, o_ref, st_ref, cos_ref, sin_ref,
                intra_ref):
    tb = ODD_ROWS
    c_len = RET_CHUNK
    t_f = lax.broadcasted_iota(jnp.int32, (c_len, 1), 0).astype(F32)

    @pl.when(pl.program_id(0) == 0)
    def _():
        st_ref[...] = jnp.zeros_like(st_ref)
        ang = lax.broadcasted_iota(jnp.int32, (tb, 1), 0).astype(F32) * inv_ref[...]
        cos_ref[...] = jnp.cos(ang)
        sin_ref[...] = jnp.sin(ang)
        rel = (lax.broadcasted_iota(jnp.int32, (c_len, c_len), 0) -
               lax.broadcasted_iota(jnp.int32, (c_len, c_len), 1)).astype(F32)
        for h in range(RET_HEADS):
            intra_ref[h] = jnp.where(rel >= 0, jnp.exp(log_g[h] * jnp.maximum(rel, 0.0)), 0.0)

    base = (pl.program_id(0) * tb).astype(F32) * inv_ref[...]
    cos_a, sin_a = jnp.cos(base), jnp.sin(base)
    cos_b, sin_b = cos_ref[...], sin_ref[...]
    cos = cos_a * cos_b - sin_a * sin_b
    sin = sin_a * cos_b + cos_a * sin_b
    half = RET_DK // 2
    qk_w = RET_HEADS * RET_DK
    v_w = RET_HEADS * RET_DV

    def rot(xh):
        x1, x2 = xh[:, :half], xh[:, half:]
        return jnp.concatenate([x1 * cos - x2 * sin, x2 * cos + x1 * sin], axis=1)

    x = h_ref[...]
    hn = _rms(x, gpre_ref[...]).astype(BF16)
    proj = lambda start, width: jnp.dot(hn, win_ref[:, start:start + width], preferred_element_type=F32)

    m = jnp.zeros(x.shape, F32)
    for h in range(RET_HEADS):
        lg = log_g[h]
        q = rot(proj(h * RET_DK, RET_DK)).astype(BF16)
        k = rot(proj(qk_w + h * RET_DK, RET_DK)) * (RET_DK ** -0.5)
        v = proj(2 * qk_w + h * RET_DV, RET_DV).astype(BF16)
        gate = proj(2 * qk_w + v_w + h * RET_DV, RET_DV)
        st = st_ref[h]
        y_chunks = []
        for c in range(tb // c_len):
            rows = slice(c * c_len, (c + 1) * c_len)
            s = _bdot_nt(q[rows], k[rows]) * intra_ref[h]
            y_chunks.append(_bdot(s, v[rows]) + _bdot(q[rows], st) * jnp.exp(lg * (t_f + 1.0)))
            st = st * math.exp(lg * c_len) + _bdot_tn(k[rows] * jnp.exp(lg * (c_len - 1.0 - t_f)), v[rows])
        st_ref[h] = st
        y = jnp.concatenate(y_chunks, axis=0)
        y = y * lax.rsqrt(jnp.mean(y * y, axis=-1, keepdims=True) + RET_EPS)
        m = m + _bdot(jax.nn.silu(gate) * y, wout_ref[h * RET_DV:(h + 1) * RET_DV, :])
    o_ref[...] = x + _rms(m, gpost_ref[...])


def _odd_mixer(h, gpre, win, inv_freq, wout, gpost, layer):
    t, d = h.shape
    blk = pl.BlockSpec((ODD_ROWS, d), lambda i: (i, 0))
    log_g = tuple(math.log1p(-(2.0 ** (-5.0 - hd))) for hd in range(RET_HEADS))
    consts = [gpre, win, inv_freq, wout, gpost]
    return pl.pallas_call(
        functools.partial(_odd_kernel, log_g),
        grid=(t // ODD_ROWS,),
        in_specs=[blk, _const_spec(gpre), _layer_spec(win, layer), _const_spec(inv_freq), _layer_spec(wout, layer),
                  _const_spec(gpost)],
        out_specs=blk,
        out_shape=jax.ShapeDtypeStruct((t, d), F32),
        scratch_shapes=[
            pltpu.VMEM((RET_HEADS, RET_DK, RET_DV), F32),
            pltpu.VMEM((ODD_ROWS, RET_DK // 2), F32),
            pltpu.VMEM((ODD_ROWS, RET_DK // 2), F32),
            pltpu.VMEM((RET_HEADS, RET_CHUNK, RET_CHUNK), F32),
        ],
        compiler_params=_params(),
        name="retention_mixer",
    )(h, *consts)


def kernel(x, norm_mix_pre, norm_mix_post, norm_ffn_pre, norm_ffn_post, ffn_w_in, ffn_w_out, ab_w_in, ab_w_out, lru_conv_w, lru_conv_b, lru_wa, lru_ba, lru_wx, lru_bx, lru_lambda, rwkv_mu, rwkv_w0, rwkv_w2, rwkv_a0, rwkv_a2, rwkv_g2, rwkv_k_k, rwkv_k_a, rwkv_r_k, rwkv_ln_w, rwkv_ln_b, ret_w_in, ret_w_out):
    assert x.shape == (1, 16384, D_MODEL)
    depth = norm_mix_pre.shape[0]
    row = lambda vec: vec.reshape(1, -1)
    seg = jnp.kron(jnp.eye(SEG_WIDTH // RWKV_HEAD, dtype=F32), jnp.ones((RWKV_HEAD, RWKV_HEAD), F32)).astype(BF16)
    masks = _pair_masks()
    half = RET_DK // 2
    inv_freq = (1.0 / (ROPE_BASE ** (jnp.arange(half, dtype=F32) / half))).reshape(1, half)
    ffn_w_in = ffn_w_in.astype(BF16)
    ffn_w_out = ffn_w_out.astype(BF16)
    ab_w_in = ab_w_in.astype(BF16)
    ab_w_out = ab_w_out.astype(BF16)
    ret_w_in = ret_w_in.astype(BF16)
    ret_w_out = ret_w_out.astype(BF16)

    h = x[0]
    for layer in range(depth):
        g_pre = row(norm_mix_pre[layer])
        g_post = row(norm_mix_post[layer])
        if layer % 2 == 0:
            e = layer // 2
            wa_bd = jax.scipy.linalg.block_diag(*lru_wa[e]).astype(BF16)
            wx_bd = jax.scipy.linalg.block_diag(*lru_wx[e]).astype(BF16)
            lru_args = [lru_conv_w[e], row(lru_conv_b[e]), wa_bd, row(lru_ba[e]), wx_bd, row(lru_bx[e]),
                        row(lru_lambda[e])]
            rwkv_args = [row(rwkv_mu[e]), row(rwkv_w0[e]), rwkv_w2[e].astype(BF16), row(rwkv_a0[e]),
                         rwkv_a2[e].astype(BF16), rwkv_g2[e].astype(BF16), row(rwkv_k_k[e]), row(rwkv_k_a[e]),
                         row(rwkv_r_k[e]), row(rwkv_ln_w[e]), row(rwkv_ln_b[e]), seg, masks]
            assert len(lru_args) == N_LRU_REFS and len(rwkv_args) == N_RWKV_REFS
            h = _even_mixer(h, g_pre, ab_w_in, lru_args, rwkv_args, ab_w_out, g_post, e)
        else:
            h = _odd_mixer(h, g_pre, ret_w_in, inv_freq, ret_w_out, g_post, layer // 2)
        h = _ffn(h, row(norm_ffn_pre[layer]), ffn_w_in, ffn_w_out, row(norm_ffn_post[layer]), layer)
    return h[None]
```

```python
import functools
import math

import jax
import jax.numpy as jnp
import numpy as np
from jax import lax
from jax.experimental import pallas as pl
from jax.experimental.pallas import tpu as pltpu

F32 = jnp.float32
BF16 = jnp.bfloat16

D_MODEL = 1024
D_FF = 4 * D_MODEL
NORM_EPS = 1e-6

LRU_WIDTH = 512
LRU_C = 8.0

RWKV_WIDTH = 512
RWKV_HEAD = 64
RWKV_LN_EPS = 64e-5
KK_NORM_FLOOR = 1e-12
RWKV_COLS = 3 * RWKV_WIDTH + 64 + 64 + 128
RWKV_CHUNK = 64
RWKV_PAIR = 2 * RWKV_HEAD
SEG_WIDTH = 256

RET_HEADS = 4
RET_DK = 256
RET_DV = 512
RET_EPS = 1e-6
ROPE_BASE = 10000.0

MIX_ROWS = 256
RET_CHUNK = 256
ODD_ROWS = 512
MM_ROWS = 512
FF_CHUNK = 1024
VMEM_LIMIT = 56 * 1024 * 1024


def _bdot(a, b):
    return jnp.dot(a.astype(BF16), b.astype(BF16), preferred_element_type=F32)


def _bdot_nt(a, b):
    return lax.dot_general(a.astype(BF16), b.astype(BF16), (((1,), (1,)), ((), ())), preferred_element_type=F32)


def _bdot_tn(a, b):
    return lax.dot_general(a.astype(BF16), b.astype(BF16), (((0,), (0,)), ((), ())), preferred_element_type=F32)


def _split(x):
    hi = x.astype(BF16)
    lo = (x - hi.astype(F32)).astype(BF16)
    return hi, lo


def _dot_exact_lhs(a_bf16, b):
    b0 = b.astype(BF16)
    r1 = b - b0.astype(F32)
    b1 = r1.astype(BF16)
    b2 = (r1 - b1.astype(F32)).astype(BF16)
    d = lambda y: jnp.dot(a_bf16, y, preferred_element_type=F32)
    return d(b0) + (d(b1) + d(b2))


def _rms(x, g):
    return x * lax.rsqrt(jnp.mean(x * x, axis=-1, keepdims=True) + NORM_EPS) * g


def _softplus(x):
    return jnp.maximum(x, 0.0) + jnp.log(1.0 + jnp.exp(-jnp.abs(x)))


def _params():
    return pltpu.CompilerParams(dimension_semantics=("arbitrary",), vmem_limit_bytes=VMEM_LIMIT)


def _const_spec(arr):
    return pl.BlockSpec(arr.shape, lambda i: (0,) * arr.ndim)


class _Cast:
    def __init__(self, stack, layer, n_blocks):
        rows, cols = stack.shape[1:]
        slab = rows // n_blocks
        assert slab * n_blocks == rows and slab % 16 == 0
        self.operand = stack
        self.in_spec = pl.BlockSpec((None, slab, cols), lambda i: (layer, jnp.minimum(i, n_blocks - 1), 0))
        self.out_spec = pl.BlockSpec((slab, cols), lambda i: (jnp.minimum(i, n_blocks - 1), 0))
        self.out_shape = jax.ShapeDtypeStruct((rows, cols), BF16)


def _run_casts(in_refs, out_refs):
    for src, dst in zip(in_refs, out_refs):
        dst[...] = src[...].astype(BF16)


N_MIXER_CASTS = 2


def _split_refs(refs, n_first, n_second):
    return refs[:n_first], refs[n_first:n_first + n_second], refs[n_first + n_second:]


def _ffn_kernel(n_casts, h_ref, gpre_ref, win_ref, wout_ref, gpost_ref, *refs):
    cast_in, o_ref, cast_out = refs[:n_casts], refs[n_casts], refs[n_casts + 1:]
    _run_casts(cast_in, cast_out)
    h = h_ref[...]
    xn = _rms(h, gpre_ref[...]).astype(BF16)
    acc = jnp.zeros(h.shape, F32)
    for c in range(D_FF // FF_CHUNK):
        cols = slice(c * FF_CHUNK, (c + 1) * FF_CHUNK)
        a = jnp.dot(xn, win_ref[:, cols], preferred_element_type=F32)
        a = jnp.square(jnp.maximum(a, 0.0))
        acc = acc + jnp.dot(a.astype(BF16), wout_ref[cols, :], preferred_element_type=F32)
    o_ref[...] = h + _rms(acc, gpost_ref[...])


def _ffn(h, gpre, win, wout, gpost, casts):
    t, d = h.shape
    blk = pl.BlockSpec((MM_ROWS, d), lambda i: (i, 0))
    return pl.pallas_call(
        functools.partial(_ffn_kernel, len(casts)),
        grid=(t // MM_ROWS,),
        in_specs=([blk, _const_spec(gpre), _const_spec(win), _const_spec(wout), _const_spec(gpost)] +
                  [c.in_spec for c in casts]),
        out_specs=[blk] + [c.out_spec for c in casts],
        out_shape=[jax.ShapeDtypeStruct((t, d), F32)] + [c.out_shape for c in casts],
        compiler_params=_params(),
        name="ffn",
    )(h, gpre, win, wout, gpost, *[c.operand for c in casts])


def _causal_conv(x, cw, cb, xc_ref, cols):
    tb = x.shape[0]
    row = lax.broadcasted_iota(jnp.int32, (tb, 1), 0)
    xc = xc_ref[:, cols]
    u = x * cw[3:4, :] + cb
    for d in (1, 2, 3):
        head = jnp.concatenate([pltpu.roll(xc, d, axis=0)] * (tb // 8), axis=0)
        shifted = jnp.where(row < d, head, pltpu.roll(x, d, axis=0))
        u = u + shifted * cw[3 - d:4 - d, :]
    xc_ref[:, cols] = x[tb - 8:tb, :]
    return u


def _token_shift(x, mu, carry_ref, cols):
    tb = x.shape[0]
    row = lax.broadcasted_iota(jnp.int32, (tb, 1), 0)
    prev = jnp.where(row == 0, carry_ref[0:1, cols], pltpu.roll(x, 1, axis=0))
    carry_ref[0:1, cols] = x[tb - 1:tb, :]
    return x + (prev - x) * mu


def _diag_tiles_dot(x, w_ref):
    parts = [_bdot(x[:, c:c + SEG_WIDTH], w_ref[c:c + SEG_WIDTH, c:c + SEG_WIDTH])
             for c in range(0, x.shape[1], SEG_WIDTH)]
    return jnp.concatenate(parts, axis=1)


def _lru_block(u, gate_in, wa_ref, ba_ref, wx_ref, bx_ref, lam_ref, hc_ref):
    tb = u.shape[0]
    row = lax.broadcasted_iota(jnp.int32, (tb, 1), 0)
    r = jax.nn.sigmoid(_diag_tiles_dot(u, wa_ref) + ba_ref[...])
    ig = jax.nn.sigmoid(_diag_tiles_dot(u, wx_ref) + bx_ref[...])
    log_a = (-LRU_C) * r * _softplus(-lam_ref[...])
    a = jnp.exp(log_a)
    b = jnp.sqrt(-jnp.tanh(log_a) * (a * a + 1.0)) * (ig * u)

    d = 1
    while d < tb:
        if d < 8:
            keep = row >= d
            a_sh = jnp.where(keep, pltpu.roll(a, d, axis=0), 1.0)
            b_sh = jnp.where(keep, pltpu.roll(b, d, axis=0), 0.0)
        else:
            a_sh = jnp.concatenate([jnp.ones((d, a.shape[1]), F32), a[:tb - d]], axis=0)
            b_sh = jnp.concatenate([jnp.zeros((d, a.shape[1]), F32), b[:tb - d]], axis=0)
        b = a * b_sh + b
        a = a * a_sh
        d *= 2
    h = a * hc_ref[0:1, :] + b
    hc_ref[0:1, :] = h[tb - 1:tb, :]
    return h * jax.nn.gelu(gate_in)


def _head_sums(x, seg_ref):
    seg = seg_ref[...]
    parts = [_bdot(x[:, c:c + SEG_WIDTH], seg) for c in range(0, x.shape[1], SEG_WIDTH)]
    return jnp.concatenate(parts, axis=1)


def _rwkv_block(xs, w0_ref, w2_ref, a0_ref, a2_ref, g2_ref, kk_ref, ka_ref, rk_ref, lnw_ref, lnb_ref,
                seg_ref, masks_ref, st_ref, fill):
    tb = xs.shape[0]
    c_len = RWKV_CHUNK
    pw = RWKV_PAIR
    n_pairs = RWKV_WIDTH // pw
    n_chunks = tb // c_len
    w = RWKV_WIDTH

    r = xs[:, 0:w]
    k = xs[:, w:2 * w]
    v = xs[:, 2 * w:3 * w]
    dw = xs[:, 3 * w:3 * w + 64]
    da = xs[:, 3 * w + 64:3 * w + 128]
    dg = xs[:, 3 * w + 128:3 * w + 256]
    fill()
    w_log = -_softplus(-(w0_ref[...] + _bdot(jnp.tanh(dw), w2_ref[...]))) - 0.5
    lw = -jnp.exp(w_log)
    fill()
    a = jax.nn.sigmoid(a0_ref[...] + _bdot(da, a2_ref[...]))
    g = _bdot(jax.nn.sigmoid(dg), g2_ref[...])
    fill()
    kk = k * kk_ref[...]
    kk = kk * lax.rsqrt(jnp.maximum(_head_sums(kk * kk, seg_ref), KK_NORM_FLOOR * KK_NORM_FLOOR))
    k = k * (1.0 + (a - 1.0) * ka_ref[...])
    z = -kk
    b = kk * a

    m_strict, m_incl, eye = masks_ref[0], masks_ref[1], masks_ref[2]
    lo_lane = lax.broadcasted_iota(jnp.int32, (c_len, pw), 1) < RWKV_HEAD
    tri = jnp.where(lax.broadcasted_iota(jnp.int32, (c_len, c_len), 0) >=
                    lax.broadcasted_iota(jnp.int32, (c_len, c_len), 1), 1.0, 0.0).astype(BF16)
    zeros_pp = jnp.zeros((pw, pw), BF16)

    def stack(xp):
        return jnp.concatenate([jnp.where(lo_lane, xp, 0.0), jnp.where(lo_lane, 0.0, xp)], axis=0)

    def stack16(xp):
        xb = xp.astype(BF16)
        zero = jnp.zeros_like(xb)
        return jnp.concatenate([jnp.where(lo_lane, xb, zero), jnp.where(lo_lane, zero, xb)], axis=0)

    cat0 = lambda *parts: jnp.concatenate(parts, axis=0)
    cat1 = lambda *parts: jnp.concatenate(parts, axis=1)

    zt, rt, bt, kt, zb, rb, bh, kh, vs, g_end = ([] for _ in range(10))
    for c in range(n_chunks):
        rows = slice(c * c_len, (c + 1) * c_len)
        lw_c = lw[rows]
        fill()
        cum = _dot_exact_lhs(tri, lw_c)
        cum_prev = cum - lw_c
        c0 = cum[c_len // 2 - 1:c_len // 2, :]
        c_end = cum[c_len - 1:c_len, :]
        e_abs = jnp.exp(cum)
        e_abs_prev = jnp.exp(cum_prev)
        inv_abs = 1.0 / e_abs
        inv_c0 = jnp.exp(-c0)
        ge_c = jnp.exp(c_end)
        e_bwd = jnp.exp(c0) * inv_abs
        e_end = ge_c * inv_abs
        zt_c, rt_c = z[rows] * (e_abs_prev * inv_c0), r[rows] * (e_abs * inv_c0)
        bt_c, kt_c = b[rows] * e_bwd, k[rows] * e_bwd
        zb_c, rb_c = z[rows] * e_abs_prev, r[rows] * e_abs
        bh_c, kh_c = b[rows] * e_end, k[rows] * e_end
        v_c = v[rows]
        for p in range(n_pairs):
            s = slice(p * pw, (p + 1) * pw)
            zt.append(stack16(zt_c[:, s]))
            rt.append(stack16(rt_c[:, s]))
            bt.append(stack16(bt_c[:, s]))
            kt.append(stack16(kt_c[:, s]))
            zb.append(stack16(zb_c[:, s]))
            rb.append(stack(rb_c[:, s]))
            bh.append(stack16(bh_c[:, s]))
            kh.append(stack16(kh_c[:, s]))
            vs.append(stack16(v_c[:, s]))
            g_end.append(ge_c[:, s])
    each = range(n_chunks * n_pairs)

    lm = []
    for i in each:
        if i % n_pairs == 0:
            fill()
        lm.append(_bdot_nt(cat0(zt[i], rt[i]), cat0(bt[i], kt[i])))
    l_bz = [lm[i][0:pw, 0:pw] * m_strict for i in each]
    l_kz = [lm[i][0:pw, pw:2 * pw] * m_strict for i in each]
    m_rbk = [cat1(lm[i][pw:2 * pw, 0:pw] * m_incl, lm[i][pw:2 * pw, pw:2 * pw] * m_incl).astype(BF16) for i in each]

    pwr = [_bdot(l_bz[i], l_bz[i]) for i in each]
    tm = [eye + l_bz[i] for i in each]
    for it in range(int(math.log2(c_len)) - 2):
        res = [_bdot(pwr[i], cat1(pwr[i], tm[i])) for i in each]
        pwr = [res[i][:, 0:pw] for i in each]
        tm = [tm[i] + res[i][:, pw:2 * pw] for i in each]
        if it == 0:
            yield None
    tm = [tm[i] + _bdot(pwr[i], tm[i]) for i in each]

    lv = [_bdot(l_kz[i], vs[i]) for i in each]
    xw = [_bdot(tm[i], cat1(zb[i], lv[i].astype(BF16))).astype(BF16) for i in each]
    rhs = [cat0(xw[i], cat1(zeros_pp, vs[i])) for i in each]
    gh = [_bdot_tn(cat0(bh[i], kh[i]), rhs[i]) for i in each]
    qy = [jnp.dot(m_rbk[i], rhs[i], preferred_element_type=F32) for i in each]

    st = [st_ref[p] for p in range(n_pairs)]
    y_rows = []
    for c in range(n_chunks):
        y_cols = []
        for p in range(n_pairs):
            i = c * n_pairs + p
            g_mat = eye * g_end[i] + gh[i][:, 0:pw]
            q_mat = rb[i] + qy[i][:, 0:pw]
            lhs = cat0(q_mat, g_mat).astype(BF16)
            st_hi, st_lo = _split(st[p])
            upd = jnp.dot(cat1(lhs, lhs), cat0(st_hi, st_lo), preferred_element_type=F32)
            y_st = upd[0:pw, :] + qy[i][:, pw:2 * pw]
            st[p] = upd[pw:2 * pw, :] + gh[i][:, pw:2 * pw]
            y_cols.append(y_st[0:c_len, :] + y_st[c_len:2 * c_len, :])
        y_rows.append(cat1(*y_cols))
    for p in range(n_pairs):
        st_ref[p] = st[p]
    y = cat0(*y_rows)

    inv_n = 1.0 / RWKV_HEAD
    mean = _head_sums(y, seg_ref) * inv_n
    dev = y - mean
    var = _head_sums(dev * dev, seg_ref) * inv_n
    yn = dev * lax.rsqrt(var + RWKV_LN_EPS) * lnw_ref[...] + lnb_ref[...]
    bonus = _head_sums(r * k * rk_ref[...], seg_ref) * v
    yield (yn + bonus) * g


N_SHIFT_REFS = 3
N_LRU_REFS = 5
N_RWKV_REFS = 12


def _pair_masks():
    t = np.arange(RWKV_PAIR) % RWKV_HEAD
    strict = t[:, None] > t[None, :]
    incl = t[:, None] >= t[None, :]
    return jnp.asarray(np.stack([strict, incl, np.eye(RWKV_PAIR, dtype=bool)]).astype(np.float32))


PROJ_PIECE = 256
EVEN_PIPELINE_DEPTH = 1


def _in_proj_pieces(h_ref, gpre_ref, win_ref, cw_ref, cb_ref, mu_ref, xc_ref, carry_ref, u_ref):
    hn = _rms(h_ref[...], gpre_ref[...]).astype(BF16)
    for c in range(0, u_ref.shape[1], PROJ_PIECE):
        piece = jnp.dot(hn, win_ref[:, c:c + PROJ_PIECE], preferred_element_type=F32)
        if c < LRU_WIDTH:
            cols = slice(c, c + PROJ_PIECE)
            piece = _causal_conv(piece, cw_ref[:, cols], cb_ref[:, cols], xc_ref, cols)
        elif c >= 2 * LRU_WIDTH:
            cols = slice(c - 2 * LRU_WIDTH, c - 2 * LRU_WIDTH + PROJ_PIECE)
            piece = _token_shift(piece, mu_ref[:, cols], carry_ref, cols)
        u_ref[:, c:c + PROJ_PIECE] = piece
        yield


def _even_kernel(h_next_ref, h_ref, gpre_ref, win_ref, *refs):
    cw_ref, cb_ref, mu_ref = refs[:N_SHIFT_REFS]
    lru_refs = refs[N_SHIFT_REFS:N_SHIFT_REFS + N_LRU_REFS]
    rwkv_refs = refs[N_SHIFT_REFS + N_LRU_REFS:N_SHIFT_REFS + N_LRU_REFS + N_RWKV_REFS]
    (wout_ref, gpost_ref, *cast_in), (o_ref, *cast_out), (xc_ref, hc_ref, carry_ref, st_ref, u0_ref, u1_ref) = (
        _split_refs(refs[N_SHIFT_REFS + N_LRU_REFS + N_RWKV_REFS:], 2 + N_MIXER_CASTS, 1 + N_MIXER_CASTS))
    step = pl.program_id(0)
    _run_casts(cast_in, cast_out)

    @pl.when(step == 0)
    def _():
        for ref in (xc_ref, hc_ref, carry_ref, st_ref, u0_ref, u1_ref):
            ref[...] = jnp.zeros_like(ref)

    def run(cur_ref, nxt_ref):
        proj = _in_proj_pieces(h_next_ref, gpre_ref, win_ref, cw_ref, cb_ref, mu_ref, xc_ref, carry_ref, nxt_ref)
        fill = lambda: next(proj, None)
        u = cur_ref[...]
        rwkv = _rwkv_block(u[:, 2 * LRU_WIDTH:], *rwkv_refs, st_ref, fill)
        next(rwkv)
        ya = _lru_block(u[:, :LRU_WIDTH], u[:, LRU_WIDTH:2 * LRU_WIDTH], *lru_refs, hc_ref)
        yb = next(rwkv)
        for _ in proj:
            pass
        m = _bdot(ya, wout_ref[:LRU_WIDTH, :]) + _bdot(yb, wout_ref[LRU_WIDTH:, :])
        o_ref[...] = h_ref[...] + _rms(m, gpost_ref[...])

    parity = lax.rem(step, 2)

    @pl.when(parity == 0)
    def _():
        run(u1_ref, u0_ref)

    @pl.when(parity == 1)
    def _():
        run(u0_ref, u1_ref)


def _even_mixer(h, gpre, win, shift_args, lru_args, rwkv_args, wout, gpost, cast_stacks, layer):
    t, d = h.shape
    n_blocks = t // MIX_ROWS
    nxt = pl.BlockSpec((MIX_ROWS, d), lambda i: (jnp.minimum(i, n_blocks - 1), 0))
    old = pl.BlockSpec((MIX_ROWS, d), lambda i: (jnp.maximum(i - EVEN_PIPELINE_DEPTH, 0), 0))
    small = [*shift_args, *lru_args, *rwkv_args]
    u_buf = pltpu.VMEM((MIX_ROWS, win.shape[1]), F32)
    casts = [_Cast(stack, layer, n_blocks) for stack in cast_stacks]
    assert len(casts) == N_MIXER_CASTS
    return pl.pallas_call(
        _even_kernel,
        grid=(n_blocks + EVEN_PIPELINE_DEPTH,),
        in_specs=([nxt, old, _const_spec(gpre), _const_spec(win)] + [_const_spec(c) for c in small] +
                  [_const_spec(wout), _const_spec(gpost)] + [c.in_spec for c in casts]),
        out_specs=[old] + [c.out_spec for c in casts],
        out_shape=[jax.ShapeDtypeStruct((t, d), F32)] + [c.out_shape for c in casts],
        scratch_shapes=[
            pltpu.VMEM((8, LRU_WIDTH), F32),
            pltpu.VMEM((8, LRU_WIDTH), F32),
            pltpu.VMEM((8, RWKV_COLS), F32),
            pltpu.VMEM((RWKV_WIDTH // RWKV_PAIR, RWKV_PAIR, RWKV_PAIR), F32),
            u_buf, u_buf,
        ],
        compiler_params=_params(),
        name="lru_rwkv_mixer",
    )(h, h, gpre, win, *small, wout, gpost, *[c.operand for c in casts])


def _odd_kernel(log_g, h_ref, gpre_ref, win_ref, inv_ref, wout_ref, gpost_ref, *refs):
    cast_in, (o_ref, *cast_out), (st_ref, cos_ref, sin_ref, intra_ref) = _split_refs(
        refs, N_MIXER_CASTS, 1 + N_MIXER_CASTS)
    _run_casts(cast_in, cast_out)
    tb = ODD_ROWS
    c_len = RET_CHUNK
    t_f = lax.broadcasted_iota(jnp.int32, (c_len, 1), 0).astype(F32)

    @pl.when(pl.program_id(0) == 0)
    def _():
        st_ref[...] = jnp.zeros_like(st_ref)
        ang = lax.broadcasted_iota(jnp.int32, (tb, 1), 0).astype(F32) * inv_ref[...]
        cos_ref[...] = jnp.cos(ang)
        sin_ref[...] = jnp.sin(ang)
        rel = (lax.broadcasted_iota(jnp.int32, (c_len, c_len), 0) -
               lax.broadcasted_iota(jnp.int32, (c_len, c_len), 1)).astype(F32)
        for h in range(RET_HEADS):
            intra_ref[h] = jnp.where(rel >= 0, jnp.exp(log_g[h] * jnp.maximum(rel, 0.0)), 0.0)

    base = (pl.program_id(0) * tb).astype(F32) * inv_ref[...]
    cos_a, sin_a = jnp.cos(base), jnp.sin(base)
    cos_b, sin_b = cos_ref[...], sin_ref[...]
    cos = cos_a * cos_b - sin_a * sin_b
    sin = sin_a * cos_b + cos_a * sin_b
    half = RET_DK // 2
    qk_w = RET_HEADS * RET_DK
    v_w = RET_HEADS * RET_DV

    def rot(xh):
        x1, x2 = xh[:, :half], xh[:, half:]
        return jnp.concatenate([x1 * cos - x2 * sin, x2 * cos + x1 * sin], axis=1)

    x = h_ref[...]
    hn = _rms(x, gpre_ref[...]).astype(BF16)
    proj = lambda start, width: jnp.dot(hn, win_ref[:, start:start + width], preferred_element_type=F32)

    m = jnp.zeros(x.shape, F32)
    for h in range(RET_HEADS):
        lg = log_g[h]
        q = rot(proj(h * RET_DK, RET_DK)).astype(BF16)
        k = rot(proj(qk_w + h * RET_DK, RET_DK)) * (RET_DK ** -0.5)
        v = proj(2 * qk_w + h * RET_DV, RET_DV).astype(BF16)
        gate = proj(2 * qk_w + v_w + h * RET_DV, RET_DV)
        st = st_ref[h]
        y_chunks = []
        for c in range(tb // c_len):
            rows = slice(c * c_len, (c + 1) * c_len)
            s = _bdot_nt(q[rows], k[rows]) * intra_ref[h]
            y_chunks.append(_bdot(s, v[rows]) + _bdot(q[rows], st) * jnp.exp(lg * (t_f + 1.0)))
            st = st * math.exp(lg * c_len) + _bdot_tn(k[rows] * jnp.exp(lg * (c_len - 1.0 - t_f)), v[rows])
        st_ref[h] = st
        y = jnp.concatenate(y_chunks, axis=0)
        y = y * lax.rsqrt(jnp.mean(y * y, axis=-1, keepdims=True) + RET_EPS)
        m = m + _bdot(jax.nn.silu(gate) * y, wout_ref[h * RET_DV:(h + 1) * RET_DV, :])
    o_ref[...] = x + _rms(m, gpost_ref[...])


def _odd_mixer(h, gpre, win, inv_freq, wout, gpost, cast_stacks, layer):
    t, d = h.shape
    blk = pl.BlockSpec((ODD_ROWS, d), lambda i: (i, 0))
    log_g = tuple(math.log1p(-(2.0 ** (-5.0 - hd))) for hd in range(RET_HEADS))
    consts = [gpre, win, inv_freq, wout, gpost]
    casts = [_Cast(stack, layer, t // ODD_ROWS) for stack in cast_stacks]
    assert len(casts) == N_MIXER_CASTS
    return pl.pallas_call(
        functools.partial(_odd_kernel, log_g),
        grid=(t // ODD_ROWS,),
        in_specs=[blk] + [_const_spec(c) for c in consts] + [c.in_spec for c in casts],
        out_specs=[blk] + [c.out_spec for c in casts],
        out_shape=[jax.ShapeDtypeStruct((t, d), F32)] + [c.out_shape for c in casts],
        scratch_shapes=[
            pltpu.VMEM((RET_HEADS, RET_DK, RET_DV), F32),
            pltpu.VMEM((ODD_ROWS, RET_DK // 2), F32),
            pltpu.VMEM((ODD_ROWS, RET_DK // 2), F32),
            pltpu.VMEM((RET_HEADS, RET_CHUNK, RET_CHUNK), F32),
        ],
        compiler_params=_params(),
        name="retention_mixer",
    )(h, *consts, *[c.operand for c in casts])


def kernel(x, norm_mix_pre, norm_mix_post, norm_ffn_pre, norm_ffn_post, ffn_w_in, ffn_w_out, ab_w_in, ab_w_out, lru_conv_w, lru_conv_b, lru_wa, lru_ba, lru_wx, lru_bx, lru_lambda, rwkv_mu, rwkv_w0, rwkv_w2, rwkv_a0, rwkv_a2, rwkv_g2, rwkv_k_k, rwkv_k_a, rwkv_r_k, rwkv_ln_w, rwkv_ln_b, ret_w_in, ret_w_out):
    assert x.shape == (1, 16384, D_MODEL)
    depth = norm_mix_pre.shape[0]
    row = lambda vec: vec.reshape(1, -1)
    seg = jnp.kron(jnp.eye(SEG_WIDTH // RWKV_HEAD, dtype=F32), jnp.ones((RWKV_HEAD, RWKV_HEAD), F32)).astype(BF16)
    masks = _pair_masks()
    half = RET_DK // 2
    inv_freq = (1.0 / (ROPE_BASE ** (jnp.arange(half, dtype=F32) / half))).reshape(1, half)
    mix_w_in, mix_w_out = ab_w_in[0].astype(BF16), ab_w_out[0].astype(BF16)
    n_ffn_blocks = x.shape[1] // MM_ROWS

    h = x[0]
    for layer in range(depth):
        g_pre = row(norm_mix_pre[layer])
        g_post = row(norm_mix_post[layer])
        ffn_stacks = [ffn_w_in, ffn_w_out]
        if layer % 2 == 0:
            e = layer // 2
            wa_bd = jax.scipy.linalg.block_diag(*lru_wa[e]).astype(BF16)
            wx_bd = jax.scipy.linalg.block_diag(*lru_wx[e]).astype(BF16)
            shift_args = [lru_conv_w[e], row(lru_conv_b[e]), row(rwkv_mu[e])]
            lru_args = [wa_bd, row(lru_ba[e]), wx_bd, row(lru_bx[e]), row(lru_lambda[e])]
            rwkv_args = [row(rwkv_w0[e]), rwkv_w2[e].astype(BF16), row(rwkv_a0[e]),
                         rwkv_a2[e].astype(BF16), rwkv_g2[e].astype(BF16), row(rwkv_k_k[e]), row(rwkv_k_a[e]),
                         row(rwkv_r_k[e]), row(rwkv_ln_w[e]), row(rwkv_ln_b[e]), seg, masks]
            assert (len(shift_args), len(lru_args), len(rwkv_args)) == (N_SHIFT_REFS, N_LRU_REFS, N_RWKV_REFS)
            h, w_ff_in, w_ff_out = _even_mixer(h, g_pre, mix_w_in, shift_args, lru_args, rwkv_args, mix_w_out,
                                               g_post, ffn_stacks, layer)
        else:
            h, w_ff_in, w_ff_out = _odd_mixer(h, g_pre, mix_w_in, inv_freq, mix_w_out, g_post, ffn_stacks, layer)
        nxt = layer + 1
        if nxt == depth:
            next_casts = []
        elif nxt % 2 == 0:
            next_casts = [_Cast(ab_w_in, nxt // 2, n_ffn_blocks), _Cast(ab_w_out, nxt // 2, n_ffn_blocks)]
        else:
            next_casts = [_Cast(ret_w_in, nxt // 2, n_ffn_blocks), _Cast(ret_w_out, nxt // 2, n_ffn_blocks)]
        h, *next_w = _ffn(h, row(norm_ffn_pre[layer]), w_ff_in, w_ff_out, row(norm_ffn_post[layer]), next_casts)
        if next_w:
            mix_w_in, mix_w_out = next_w
    return h[None]
```

```python
import functools
import math

import jax
import jax.numpy as jnp
import numpy as np
from jax import lax
from jax.experimental import pallas as pl
from jax.experimental.pallas import tpu as pltpu

F32 = jnp.float32
BF16 = jnp.bfloat16

D_MODEL = 1024
D_FF = 4 * D_MODEL
NORM_EPS = 1e-6

LRU_WIDTH = 512
LRU_C = 8.0

RWKV_WIDTH = 512
RWKV_HEAD = 64
RWKV_LN_EPS = 64e-5
KK_NORM_FLOOR = 1e-12
RWKV_COLS = 3 * RWKV_WIDTH + 64 + 64 + 128
RWKV_CHUNK = 64
RWKV_PAIR = 2 * RWKV_HEAD
SEG_WIDTH = 256

RET_HEADS = 4
RET_DK = 256
RET_DV = 512
RET_EPS = 1e-6
ROPE_BASE = 10000.0

MIX_ROWS = 256
RET_CHUNK = 256
ODD_ROWS = 512
MM_ROWS = 512
FF_CHUNK = 1024
VMEM_LIMIT = 56 * 1024 * 1024


def _bdot(a, b):
    return jnp.dot(a.astype(BF16), b.astype(BF16), preferred_element_type=F32)


def _bdot_nt(a, b):
    return lax.dot_general(a.astype(BF16), b.astype(BF16), (((1,), (1,)), ((), ())), preferred_element_type=F32)


def _bdot_tn(a, b):
    return lax.dot_general(a.astype(BF16), b.astype(BF16), (((0,), (0,)), ((), ())), preferred_element_type=F32)


def _split(x):
    hi = x.astype(BF16)
    lo = (x - hi.astype(F32)).astype(BF16)
    return hi, lo


def _dot_exact_lhs(a_bf16, b):
    b0 = b.astype(BF16)
    r1 = b - b0.astype(F32)
    b1 = r1.astype(BF16)
    b2 = (r1 - b1.astype(F32)).astype(BF16)
    d = lambda y: jnp.dot(a_bf16, y, preferred_element_type=F32)
    return d(b0) + (d(b1) + d(b2))


def _rms(x, g):
    return x * lax.rsqrt(jnp.mean(x * x, axis=-1, keepdims=True) + NORM_EPS) * g


def _softplus(x):
    return jnp.maximum(x, 0.0) + jnp.log(1.0 + jnp.exp(-jnp.abs(x)))


def _params():
    return pltpu.CompilerParams(dimension_semantics=("arbitrary",), vmem_limit_bytes=VMEM_LIMIT)


def _const_spec(arr):
    return pl.BlockSpec(arr.shape, lambda i: (0,) * arr.ndim)


class _Cast:
    def __init__(self, stack, layer, n_blocks):
        rows, cols = stack.shape[1:]
        slab = rows // n_blocks
        assert slab * n_blocks == rows and slab % 16 == 0
        self.operand = stack
        self.in_spec = pl.BlockSpec((None, slab, cols), lambda i: (layer, jnp.minimum(i, n_blocks - 1), 0))
        self.out_spec = pl.BlockSpec((slab, cols), lambda i: (jnp.minimum(i, n_blocks - 1), 0))
        self.out_shape = jax.ShapeDtypeStruct((rows, cols), BF16)


def _run_casts(in_refs, out_refs):
    for src, dst in zip(in_refs, out_refs):
        dst[...] = src[...].astype(BF16)


N_MIXER_CASTS = 2


def _split_refs(refs, n_first, n_second):
    return refs[:n_first], refs[n_first:n_first + n_second], refs[n_first + n_second:]


def _ffn_kernel(n_casts, h_ref, gpre_ref, win_ref, wout_ref, gpost_ref, *refs):
    cast_in, o_ref, cast_out = refs[:n_casts], refs[n_casts], refs[n_casts + 1:]
    _run_casts(cast_in, cast_out)
    h = h_ref[...]
    xn = _rms(h, gpre_ref[...]).astype(BF16)
    acc = jnp.zeros(h.shape, F32)
    for c in range(D_FF // FF_CHUNK):
        cols = slice(c * FF_CHUNK, (c + 1) * FF_CHUNK)
        a = jnp.dot(xn, win_ref[:, cols], preferred_element_type=F32)
        a = jnp.square(jnp.maximum(a, 0.0))
        acc = acc + jnp.dot(a.astype(BF16), wout_ref[cols, :], preferred_element_type=F32)
    o_ref[...] = h + _rms(acc, gpost_ref[...])


def _ffn(h, gpre, win, wout, gpost, casts):
    t, d = h.shape
    blk = pl.BlockSpec((MM_ROWS, d), lambda i: (i, 0))
    return pl.pallas_call(
        functools.partial(_ffn_kernel, len(casts)),
        grid=(t // MM_ROWS,),
        in_specs=([blk, _const_spec(gpre), _const_spec(win), _const_spec(wout), _const_spec(gpost)] +
                  [c.in_spec for c in casts]),
        out_specs=[blk] + [c.out_spec for c in casts],
        out_shape=[jax.ShapeDtypeStruct((t, d), F32)] + [c.out_shape for c in casts],
        compiler_params=_params(),
        name="ffn",
    )(h, gpre, win, wout, gpost, *[c.operand for c in casts])


def _causal_conv(x, cw, cb, xc_ref, cols):
    tb = x.shape[0]
    row = lax.broadcasted_iota(jnp.int32, (tb, 1), 0)
    xc = xc_ref[:, cols]
    u = x * cw[3:4, :] + cb
    for d in (1, 2, 3):
        head = jnp.concatenate([pltpu.roll(xc, d, axis=0)] * (tb // 8), axis=0)
        shifted = jnp.where(row < d, head, pltpu.roll(x, d, axis=0))
        u = u + shifted * cw[3 - d:4 - d, :]
    xc_ref[:, cols] = x[tb - 8:tb, :]
    return u


def _token_shift(x, mu, carry_ref, cols):
    tb = x.shape[0]
    row = lax.broadcasted_iota(jnp.int32, (tb, 1), 0)
    prev = jnp.where(row == 0, carry_ref[0:1, cols], pltpu.roll(x, 1, axis=0))
    carry_ref[0:1, cols] = x[tb - 1:tb, :]
    return x + (prev - x) * mu


def _diag_tiles_dot(x, w_ref):
    parts = [_bdot(x[:, c:c + SEG_WIDTH], w_ref[c:c + SEG_WIDTH, c:c + SEG_WIDTH])
             for c in range(0, x.shape[1], SEG_WIDTH)]
    return jnp.concatenate(parts, axis=1)


def _lru_block(u, gate_in, wa_ref, ba_ref, wx_ref, bx_ref, lam_ref, hc_ref):
    tb = u.shape[0]
    row = lax.broadcasted_iota(jnp.int32, (tb, 1), 0)
    r = jax.nn.sigmoid(_diag_tiles_dot(u, wa_ref) + ba_ref[...])
    ig = jax.nn.sigmoid(_diag_tiles_dot(u, wx_ref) + bx_ref[...])
    log_a = (-LRU_C) * r * _softplus(-lam_ref[...])
    a = jnp.exp(log_a)
    b = jnp.sqrt(-jnp.tanh(log_a) * (a * a + 1.0)) * (ig * u)

    d = 1
    while d < tb:
        if d < 8:
            keep = row >= d
            a_sh = jnp.where(keep, pltpu.roll(a, d, axis=0), 1.0)
            b_sh = jnp.where(keep, pltpu.roll(b, d, axis=0), 0.0)
        else:
            a_sh = jnp.concatenate([jnp.ones((d, a.shape[1]), F32), a[:tb - d]], axis=0)
            b_sh = jnp.concatenate([jnp.zeros((d, a.shape[1]), F32), b[:tb - d]], axis=0)
        b = a * b_sh + b
        a = a * a_sh
        d *= 2
    h = a * hc_ref[0:1, :] + b
    hc_ref[0:1, :] = h[tb - 1:tb, :]
    return h * jax.nn.gelu(gate_in)


def _head_sums(x, seg_ref):
    seg = seg_ref[...]
    parts = [_bdot(x[:, c:c + SEG_WIDTH], seg) for c in range(0, x.shape[1], SEG_WIDTH)]
    return jnp.concatenate(parts, axis=1)


def _rwkv_block(xs, w0_ref, w2_ref, a0_ref, a2_ref, g2_ref, kk_ref, ka_ref, rk_ref, lnw_ref, lnb_ref,
                seg_ref, masks_ref, st_ref, fill):
    tb = xs.shape[0]
    c_len = RWKV_CHUNK
    pw = RWKV_PAIR
    n_pairs = RWKV_WIDTH // pw
    n_chunks = tb // c_len
    w = RWKV_WIDTH

    r = xs[:, 0:w]
    k = xs[:, w:2 * w]
    v = xs[:, 2 * w:3 * w]
    dw = xs[:, 3 * w:3 * w + 64]
    da = xs[:, 3 * w + 64:3 * w + 128]
    dg = xs[:, 3 * w + 128:3 * w + 256]
    fill()
    w_log = -_softplus(-(w0_ref[...] + _bdot(jnp.tanh(dw), w2_ref[...]))) - 0.5
    lw = -jnp.exp(w_log)
    fill()
    a = jax.nn.sigmoid(a0_ref[...] + _bdot(da, a2_ref[...]))
    g = _bdot(jax.nn.sigmoid(dg), g2_ref[...])
    fill()
    kk = k * kk_ref[...]
    kk = kk * lax.rsqrt(jnp.maximum(_head_sums(kk * kk, seg_ref), KK_NORM_FLOOR * KK_NORM_FLOOR))
    k = k * (1.0 + (a - 1.0) * ka_ref[...])
    z = -kk
    b = kk * a

    m_strict, m_incl, eye = masks_ref[0], masks_ref[1], masks_ref[2]
    lo_lane = lax.broadcasted_iota(jnp.int32, (c_len, pw), 1) < RWKV_HEAD
    tri = jnp.where(lax.broadcasted_iota(jnp.int32, (c_len, c_len), 0) >=
                    lax.broadcasted_iota(jnp.int32, (c_len, c_len), 1), 1.0, 0.0).astype(BF16)
    zeros_pp = jnp.zeros((pw, pw), BF16)

    def stack(xp):
        return jnp.concatenate([jnp.where(lo_lane, xp, 0.0), jnp.where(lo_lane, 0.0, xp)], axis=0)

    def stack16(xp):
        xb = xp.astype(BF16)
        zero = jnp.zeros_like(xb)
        return jnp.concatenate([jnp.where(lo_lane, xb, zero), jnp.where(lo_lane, zero, xb)], axis=0)

    cat0 = lambda *parts: jnp.concatenate(parts, axis=0)
    cat1 = lambda *parts: jnp.concatenate(parts, axis=1)

    zt, rt, bt, kt, zb, rb, bh, kh, vs, g_end = ([] for _ in range(10))
    for c in range(n_chunks):
        rows = slice(c * c_len, (c + 1) * c_len)
        lw_c = lw[rows]
        fill()
        cum = _dot_exact_lhs(tri, lw_c)
        cum_prev = cum - lw_c
        c0 = cum[c_len // 2 - 1:c_len // 2, :]
        c_end = cum[c_len - 1:c_len, :]
        e_abs = jnp.exp(cum)
        e_abs_prev = jnp.exp(cum_prev)
        inv_abs = 1.0 / e_abs
        inv_c0 = jnp.exp(-c0)
        ge_c = jnp.exp(c_end)
        e_bwd = jnp.exp(c0) * inv_abs
        e_end = ge_c * inv_abs
        zt_c, rt_c = z[rows] * (e_abs_prev * inv_c0), r[rows] * (e_abs * inv_c0)
        bt_c, kt_c = b[rows] * e_bwd, k[rows] * e_bwd
        zb_c, rb_c = z[rows] * e_abs_prev, r[rows] * e_abs
        bh_c, kh_c = b[rows] * e_end, k[rows] * e_end
        v_c = v[rows]
        for p in range(n_pairs):
            s = slice(p * pw, (p + 1) * pw)
            zt.append(stack16(zt_c[:, s]))
            rt.append(stack16(rt_c[:, s]))
            bt.append(stack16(bt_c[:, s]))
            kt.append(stack16(kt_c[:, s]))
            zb.append(stack16(zb_c[:, s]))
            rb.append(stack(rb_c[:, s]))
            bh.append(stack16(bh_c[:, s]))
            kh.append(stack16(kh_c[:, s]))
            vs.append(stack16(v_c[:, s]))
            g_end.append(ge_c[:, s])
    each = range(n_chunks * n_pairs)

    lm = []
    for i in each:
        if i % n_pairs == 0:
            fill()
        lm.append(_bdot_nt(cat0(zt[i], rt[i]), cat0(bt[i], kt[i])))
    l_bz = [lm[i][0:pw, 0:pw] * m_strict for i in each]
    l_kz = [lm[i][0:pw, pw:2 * pw] * m_strict for i in each]
    m_rbk = [cat1(lm[i][pw:2 * pw, 0:pw] * m_incl, lm[i][pw:2 * pw, pw:2 * pw] * m_incl).astype(BF16) for i in each]

    pwr = [_bdot(l_bz[i], l_bz[i]) for i in each]
    tm = [eye + l_bz[i] for i in each]
    for it in range(int(math.log2(c_len)) - 2):
        res = [_bdot(pwr[i], cat1(pwr[i], tm[i])) for i in each]
        pwr = [res[i][:, 0:pw] for i in each]
        tm = [tm[i] + res[i][:, pw:2 * pw] for i in each]
        if it == 0:
            yield None
    tm = [tm[i] + _bdot(pwr[i], tm[i]) for i in each]

    lv = [_bdot(l_kz[i], vs[i]) for i in each]
    xw = [_bdot(tm[i], cat1(zb[i], lv[i].astype(BF16))).astype(BF16) for i in each]
    rhs = [cat0(xw[i], cat1(zeros_pp, vs[i])) for i in each]
    gh = [_bdot_tn(cat0(bh[i], kh[i]), rhs[i]) for i in each]
    qy = [jnp.dot(m_rbk[i], rhs[i], preferred_element_type=F32) for i in each]

    st = [st_ref[p] for p in range(n_pairs)]
    y_rows = []
    for c in range(n_chunks):
        y_cols = []
        for p in range(n_pairs):
            i = c * n_pairs + p
            g_mat = eye * g_end[i] + gh[i][:, 0:pw]
            q_mat = rb[i] + qy[i][:, 0:pw]
            lhs = cat0(q_mat, g_mat).astype(BF16)
            st_hi, st_lo = _split(st[p])
            upd = jnp.dot(cat1(lhs, lhs), cat0(st_hi, st_lo), preferred_element_type=F32)
            y_st = upd[0:pw, :] + qy[i][:, pw:2 * pw]
            st[p] = upd[pw:2 * pw, :] + gh[i][:, pw:2 * pw]
            y_cols.append(y_st[0:c_len, :] + y_st[c_len:2 * c_len, :])
        y_rows.append(cat1(*y_cols))
    for p in range(n_pairs):
        st_ref[p] = st[p]
    y = cat0(*y_rows)

    inv_n = 1.0 / RWKV_HEAD
    mean = _head_sums(y, seg_ref) * inv_n
    dev = y - mean
    var = _head_sums(dev * dev, seg_ref) * inv_n
    yn = dev * lax.rsqrt(var + RWKV_LN_EPS) * lnw_ref[...] + lnb_ref[...]
    bonus = _head_sums(r * k * rk_ref[...], seg_ref) * v
    yield (yn + bonus) * g


N_SHIFT_REFS = 3
N_LRU_REFS = 5
N_RWKV_REFS = 12


def _pair_masks():
    t = np.arange(RWKV_PAIR) % RWKV_HEAD
    strict = t[:, None] > t[None, :]
    incl = t[:, None] >= t[None, :]
    return jnp.asarray(np.stack([strict, incl, np.eye(RWKV_PAIR, dtype=bool)]).astype(np.float32))


PROJ_PIECE = 256
EVEN_PIPELINE_DEPTH = 1


def _in_proj_pieces(h_ref, gpre_ref, win_ref, cw_ref, cb_ref, mu_ref, xc_ref, carry_ref, u_ref):
    hn = _rms(h_ref[...], gpre_ref[...]).astype(BF16)
    for c in range(0, u_ref.shape[1], PROJ_PIECE):
        piece = jnp.dot(hn, win_ref[:, c:c + PROJ_PIECE], preferred_element_type=F32)
        if c < LRU_WIDTH:
            cols = slice(c, c + PROJ_PIECE)
            piece = _causal_conv(piece, cw_ref[:, cols], cb_ref[:, cols], xc_ref, cols)
        elif c >= 2 * LRU_WIDTH:
            cols = slice(c - 2 * LRU_WIDTH, c - 2 * LRU_WIDTH + PROJ_PIECE)
            piece = _token_shift(piece, mu_ref[:, cols], carry_ref, cols)
        u_ref[:, c:c + PROJ_PIECE] = piece
        yield


def _even_kernel(h_next_ref, h_ref, gpre_ref, win_ref, *refs):
    cw_ref, cb_ref, mu_ref = refs[:N_SHIFT_REFS]
    lru_refs = refs[N_SHIFT_REFS:N_SHIFT_REFS + N_LRU_REFS]
    rwkv_refs = refs[N_SHIFT_REFS + N_LRU_REFS:N_SHIFT_REFS + N_LRU_REFS + N_RWKV_REFS]
    (wout_ref, gpost_ref, *cast_in), (o_ref, *cast_out), (xc_ref, hc_ref, carry_ref, st_ref, u0_ref, u1_ref) = (
        _split_refs(refs[N_SHIFT_REFS + N_LRU_REFS + N_RWKV_REFS:], 2 + N_MIXER_CASTS, 1 + N_MIXER_CASTS))
    step = pl.program_id(0)
    _run_casts(cast_in, cast_out)

    @pl.when(step == 0)
    def _():
        for ref in (xc_ref, hc_ref, carry_ref, st_ref):
            ref[...] = jnp.zeros_like(ref)
        for _ in _in_proj_pieces(h_next_ref, gpre_ref, win_ref, cw_ref, cb_ref, mu_ref, xc_ref, carry_ref, u0_ref):
            pass

    def run(cur_ref, nxt_ref):
        proj = _in_proj_pieces(h_next_ref, gpre_ref, win_ref, cw_ref, cb_ref, mu_ref, xc_ref, carry_ref, nxt_ref)
        fill = lambda: next(proj, None)
        u = cur_ref[...]
        rwkv = _rwkv_block(u[:, 2 * LRU_WIDTH:], *rwkv_refs, st_ref, fill)
        next(rwkv)
        ya = _lru_block(u[:, :LRU_WIDTH], u[:, LRU_WIDTH:2 * LRU_WIDTH], *lru_refs, hc_ref)
        yb = next(rwkv)
        for _ in proj:
            pass
        m = _bdot(ya, wout_ref[:LRU_WIDTH, :]) + _bdot(yb, wout_ref[LRU_WIDTH:, :])
        o_ref[...] = h_ref[...] + _rms(m, gpost_ref[...])

    parity = lax.rem(step, 2)

    @pl.when(jnp.logical_and(parity == 0, step > 0))
    def _():
        run(u1_ref, u0_ref)

    @pl.when(parity == 1)
    def _():
        run(u0_ref, u1_ref)


def _even_mixer(h, gpre, win, shift_args, lru_args, rwkv_args, wout, gpost, cast_stacks, layer):
    t, d = h.shape
    n_blocks = t // MIX_ROWS
    nxt = pl.BlockSpec((MIX_ROWS, d), lambda i: (jnp.minimum(i, n_blocks - 1), 0))
    old = pl.BlockSpec((MIX_ROWS, d), lambda i: (jnp.maximum(i - EVEN_PIPELINE_DEPTH, 0), 0))
    small = [*shift_args, *lru_args, *rwkv_args]
    u_buf = pltpu.VMEM((MIX_ROWS, win.shape[1]), F32)
    casts = [_Cast(stack, layer, n_blocks) for stack in cast_stacks]
    assert len(casts) == N_MIXER_CASTS
    return pl.pallas_call(
        _even_kernel,
        grid=(n_blocks + EVEN_PIPELINE_DEPTH,),
        in_specs=([nxt, old, _const_spec(gpre), _const_spec(win)] + [_const_spec(c) for c in small] +
                  [_const_spec(wout), _const_spec(gpost)] + [c.in_spec for c in casts]),
        out_specs=[old] + [c.out_spec for c in casts],
        out_shape=[jax.ShapeDtypeStruct((t, d), F32)] + [c.out_shape for c in casts],
        scratch_shapes=[
            pltpu.VMEM((8, LRU_WIDTH), F32),
            pltpu.VMEM((8, LRU_WIDTH), F32),
            pltpu.VMEM((8, RWKV_COLS), F32),
            pltpu.VMEM((RWKV_WIDTH // RWKV_PAIR, RWKV_PAIR, RWKV_PAIR), F32),
            u_buf, u_buf,
        ],
        compiler_params=_params(),
        name="lru_rwkv_mixer",
    )(h, h, gpre, win, *small, wout, gpost, *[c.operand for c in casts])


def _odd_kernel(log_g, h_ref, gpre_ref, win_ref, inv_ref, wout_ref, gpost_ref, *refs):
    cast_in, (o_ref, *cast_out), (st_ref, cos_ref, sin_ref, intra_ref) = _split_refs(
        refs, N_MIXER_CASTS, 1 + N_MIXER_CASTS)
    _run_casts(cast_in, cast_out)
    tb = ODD_ROWS
    c_len = RET_CHUNK
    t_f = lax.broadcasted_iota(jnp.int32, (c_len, 1), 0).astype(F32)

    @pl.when(pl.program_id(0) == 0)
    def _():
        st_ref[...] = jnp.zeros_like(st_ref)
        ang = lax.broadcasted_iota(jnp.int32, (tb, 1), 0).astype(F32) * inv_ref[...]
        cos_ref[...] = jnp.cos(ang)
        sin_ref[...] = jnp.sin(ang)
        rel = (lax.broadcasted_iota(jnp.int32, (c_len, c_len), 0) -
               lax.broadcasted_iota(jnp.int32, (c_len, c_len), 1)).astype(F32)
        for h in range(RET_HEADS):
            intra_ref[h] = jnp.where(rel >= 0, jnp.exp(log_g[h] * jnp.maximum(rel, 0.0)), 0.0)

    base = (pl.program_id(0) * tb).astype(F32) * inv_ref[...]
    cos_a, sin_a = jnp.cos(base), jnp.sin(base)
    cos_b, sin_b = cos_ref[...], sin_ref[...]
    cos = cos_a * cos_b - sin_a * sin_b
    sin = sin_a * cos_b + cos_a * sin_b
    half = RET_DK // 2
    qk_w = RET_HEADS * RET_DK
    v_w = RET_HEADS * RET_DV

    def rot(xh):
        x1, x2 = xh[:, :half], xh[:, half:]
        return jnp.concatenate([x1 * cos - x2 * sin, x2 * cos + x1 * sin], axis=1)

    x = h_ref[...]
    hn = _rms(x, gpre_ref[...]).astype(BF16)
    proj = lambda start, width: jnp.dot(hn, win_ref[:, start:start + width], preferred_element_type=F32)

    m = jnp.zeros(x.shape, F32)
    for h in range(RET_HEADS):
        lg = log_g[h]
        q = rot(proj(h * RET_DK, RET_DK)).astype(BF16)
        k = rot(proj(qk_w + h * RET_DK, RET_DK)) * (RET_DK ** -0.5)
        v = proj(2 * qk_w + h * RET_DV, RET_DV).astype(BF16)
        gate = proj(2 * qk_w + v_w + h * RET_DV, RET_DV)
        st = st_ref[h]
        y_chunks = []
        for c in range(tb // c_len):
            rows = slice(c * c_len, (c + 1) * c_len)
            s = _bdot_nt(q[rows], k[rows]) * intra_ref[h]
            y_chunks.append(_bdot(s, v[rows]) + _bdot(q[rows], st) * jnp.exp(lg * (t_f + 1.0)))
            st = st * math.exp(lg * c_len) + _bdot_tn(k[rows] * jnp.exp(lg * (c_len - 1.0 - t_f)), v[rows])
        st_ref[h] = st
        y = jnp.concatenate(y_chunks, axis=0)
        y = y * lax.rsqrt(jnp.mean(y * y, axis=-1, keepdims=True) + RET_EPS)
        m = m + _bdot(jax.nn.silu(gate) * y, wout_ref[h * RET_DV:(h + 1) * RET_DV, :])
    o_ref[...] = x + _rms(m, gpost_ref[...])


def _odd_mixer(h, gpre, win, inv_freq, wout, gpost, cast_stacks, layer):
    t, d = h.shape
    blk = pl.BlockSpec((ODD_ROWS, d), lambda i: (i, 0))
    log_g = tuple(math.log1p(-(2.0 ** (-5.0 - hd))) for hd in range(RET_HEADS))
    consts = [gpre, win, inv_freq, wout, gpost]
    casts = [_Cast(stack, layer, t // ODD_ROWS) for stack in cast_stacks]
    assert len(casts) == N_MIXER_CASTS
    return pl.pallas_call(
        functools.partial(_odd_kernel, log_g),
        grid=(t // ODD_ROWS,),
        in_specs=[blk] + [_const_spec(c) for c in consts] + [c.in_spec for c in casts],
        out_specs=[blk] + [c.out_spec for c in casts],
        out_shape=[jax.ShapeDtypeStruct((t, d), F32)] + [c.out_shape for c in casts],
        scratch_shapes=[
            pltpu.VMEM((RET_HEADS, RET_DK, RET_DV), F32),
            pltpu.VMEM((ODD_ROWS, RET_DK // 2), F32),
            pltpu.VMEM((ODD_ROWS, RET_DK // 2), F32),
            pltpu.VMEM((RET_HEADS, RET_CHUNK, RET_CHUNK), F32),
        ],
        compiler_params=_params(),
        name="retention_mixer",
    )(h, *consts, *[c.operand for c in casts])


def kernel(x, norm_mix_pre, norm_mix_post, norm_ffn_pre, norm_ffn_post, ffn_w_in, ffn_w_out, ab_w_in, ab_w_out, lru_conv_w, lru_conv_b, lru_wa, lru_ba, lru_wx, lru_bx, lru_lambda, rwkv_mu, rwkv_w0, rwkv_w2, rwkv_a0, rwkv_a2, rwkv_g2, rwkv_k_k, rwkv_k_a, rwkv_r_k, rwkv_ln_w, rwkv_ln_b, ret_w_in, ret_w_out):
    assert x.shape == (1, 16384, D_MODEL)
    depth = norm_mix_pre.shape[0]
    row = lambda vec: vec.reshape(1, -1)
    seg = jnp.kron(jnp.eye(SEG_WIDTH // RWKV_HEAD, dtype=F32), jnp.ones((RWKV_HEAD, RWKV_HEAD), F32)).astype(BF16)
    masks = _pair_masks()
    half = RET_DK // 2
    inv_freq = (1.0 / (ROPE_BASE ** (jnp.arange(half, dtype=F32) / half))).reshape(1, half)
    mix_w_in, mix_w_out = ab_w_in[0].astype(BF16), ab_w_out[0].astype(BF16)
    n_ffn_blocks = x.shape[1] // MM_ROWS

    h = x[0]
    for layer in range(depth):
        g_pre = row(norm_mix_pre[layer])
        g_post = row(norm_mix_post[layer])
        ffn_stacks = [ffn_w_in, ffn_w_out]
        if layer % 2 == 0:
            e = layer // 2
            wa_bd = jax.scipy.linalg.block_diag(*lru_wa[e]).astype(BF16)
            wx_bd = jax.scipy.linalg.block_diag(*lru_wx[e]).astype(BF16)
            shift_args = [lru_conv_w[e], row(lru_conv_b[e]), row(rwkv_mu[e])]
            lru_args = [wa_bd, row(lru_ba[e]), wx_bd, row(lru_bx[e]), row(lru_lambda[e])]
            rwkv_args = [row(rwkv_w0[e]), rwkv_w2[e].astype(BF16), row(rwkv_a0[e]),
                         rwkv_a2[e].astype(BF16), rwkv_g2[e].astype(BF16), row(rwkv_k_k[e]), row(rwkv_k_a[e]),
                         row(rwkv_r_k[e]), row(rwkv_ln_w[e]), row(rwkv_ln_b[e]), seg, masks]
            assert (len(shift_args), len(lru_args), len(rwkv_args)) == (N_SHIFT_REFS, N_LRU_REFS, N_RWKV_REFS)
            h, w_ff_in, w_ff_out = _even_mixer(h, g_pre, mix_w_in, shift_args, lru_args, rwkv_args, mix_w_out,
                                               g_post, ffn_stacks, layer)
        else:
            h, w_ff_in, w_ff_out = _odd_mixer(h, g_pre, mix_w_in, inv_freq, mix_w_out, g_post, ffn_stacks, layer)
        nxt = layer + 1
        if nxt == depth:
            next_casts = []
        elif nxt % 2 == 0:
            next_casts = [_Cast(ab_w_in, nxt // 2, n_ffn_blocks), _Cast(ab_w_out, nxt // 2, n_ffn_blocks)]
        else:
            next_casts = [_Cast(ret_w_in, nxt // 2, n_ffn_blocks), _Cast(ret_w_out, nxt // 2, n_ffn_blocks)]
        h, *next_w = _ffn(h, row(norm_ffn_pre[layer]), w_ff_in, w_ff_out, row(norm_ffn_post[layer]), next_casts)
        if next_w:
            mix_w_in, mix_w_out = next_w
    return h[None]
```
